```python
import jax, jax.numpy as jnp
from jax import lax
import numpy as np

D_MODEL = 4096
BATCH = 4
SEQ = 4096
DEPTH = 2

GLA_HEADS = 8
GLA_VAL = D_MODEL // 2
GLA_KEY = D_MODEL // 4
GLA_DK = GLA_KEY // GLA_HEADS
GLA_DV = GLA_VAL // GLA_HEADS
GATE_RANK = 16
GATE_TAU = 16.0
CHUNK = 64
CONV_WIDTH = D_MODEL // 2
CONV_GROUPS = 16
CONV_K = 3
MIX_WIDTH = GLA_VAL + CONV_WIDTH
SPLITS = tuple(np.cumsum([GLA_KEY, GLA_KEY, GLA_VAL, GLA_VAL, GATE_RANK, CONV_WIDTH, CONV_WIDTH]).tolist())
N_IN = SPLITS[-1] + CONV_WIDTH
D_FF = ((8 * D_MODEL) // 3 + 255) // 256 * 256
N_EXPERTS = 8
TOP_K = 2
D_EXPERT = D_MODEL
N_DENSE = (DEPTH + 1) // 2
N_MOE = DEPTH // 2
PLE_DIM = 256
EPS = 1e-6

kernel_name = "hybrid_gla_shortconv_moe_ple"


def rmsnorm(x, g):
    x32 = x.astype(jnp.float32)
    y = x32 * lax.rsqrt(jnp.mean(x32 * x32, axis=-1, keepdims=True) + EPS)
    return (y * g.astype(jnp.float32)).astype(x.dtype)


def group_rmsnorm(y, g, n_groups):
    shp = y.shape
    y32 = y.astype(jnp.float32).reshape(shp[:-1] + (n_groups, shp[-1] // n_groups))
    y32 = y32 * lax.rsqrt(jnp.mean(y32 * y32, axis=-1, keepdims=True) + EPS)
    return (y32.reshape(shp) * g.astype(jnp.float32)).astype(y.dtype)


def gla_chunked(q, k, v, log_a):
    bsz, seq, heads, dk = q.shape
    dv = v.shape[-1]
    n_chunks = seq // CHUNK

    def to_chunks(z):
        return z.reshape(bsz, n_chunks, CHUNK, heads, z.shape[-1]).transpose(1, 0, 3, 2, 4)

    qc, kc, vc, gc = to_chunks(q), to_chunks(k), to_chunks(v), to_chunks(log_a)
    bc = jnp.cumsum(gc, axis=3)
    causal = jnp.tril(jnp.ones((CHUNK, CHUNK), dtype=bool))[:, :, None]

    def step(state, inp):
        q_, k_, v_, b_ = inp
        diff = b_[:, :, :, None, :] - b_[:, :, None, :, :]
        decay = jnp.exp(jnp.where(causal, diff, -jnp.inf))
        scores = jnp.einsum('bhtk,bhsk,bhtsk->bhts', q_, k_, decay)
        o = jnp.einsum('bhts,bhsv->bhtv', scores, v_) \
            + jnp.einsum('bhtk,bhkv->bhtv', q_ * jnp.exp(b_), state)
        b_last = b_[:, :, -1:, :]
        state = jnp.exp(b_last[:, :, 0, :])[..., None] * state \
            + jnp.einsum('bhsk,bhsv->bhkv', k_ * jnp.exp(b_last - b_), v_)
        return state, o

    s0 = jnp.zeros((bsz, heads, dk, dv), jnp.float32)
    _, o = lax.scan(step, s0, (qc, kc, vc, bc))
    return o.transpose(1, 0, 3, 2, 4).reshape(bsz, seq, heads, dv)


def hybrid_mixer(xn, w_in, w_a_up, b_a, gla_g, conv_w, conv_g, w_out):
    bsz, seq, _ = xn.shape
    proj = xn @ w_in
    q, k, v, r, a, gate_b, gate_c, u = jnp.split(proj, SPLITS, axis=-1)
    q = q.reshape(bsz, seq, GLA_HEADS, GLA_DK).astype(jnp.float32) * (GLA_DK ** -0.5)
    k = k.reshape(bsz, seq, GLA_HEADS, GLA_DK).astype(jnp.float32)
    v = v.reshape(bsz, seq, GLA_HEADS, GLA_DV).astype(jnp.float32)
    log_a = jax.nn.log_sigmoid((a @ w_a_up + b_a).astype(jnp.float32)) / GATE_TAU
    log_a = log_a.reshape(bsz, seq, GLA_HEADS, GLA_DK)
    o = gla_chunked(q, k, v, log_a).reshape(bsz, seq, GLA_VAL)
    y_a = group_rmsnorm(o, gla_g, GLA_HEADS).astype(xn.dtype) * jax.nn.silu(r)
    cu = gate_c * u
    conv = lax.conv_general_dilated(
        cu, conv_w.astype(cu.dtype)[:, None, :], window_strides=(1,),
        padding=[(CONV_K - 1, 0)], dimension_numbers=('NWC', 'WIO', 'NWC'),
        feature_group_count=CONV_WIDTH)
    y_b = group_rmsnorm(gate_b * conv, conv_g, CONV_GROUPS)
    return jnp.concatenate([y_a, y_b], axis=-1) @ w_out


def swiglu(x, wg, wu, wd):
    return (jax.nn.silu(x @ wg) * (x @ wu)) @ wd


def moe_swiglu(xn, router_w, router_b, eg, eu, ed):
    bsz, seq, d = xn.shape
    xt = xn.reshape(-1, d)
    logits = (xt @ router_w + router_b).astype(jnp.float32)
    top_v, top_i = lax.top_k(logits, TOP_K)
    gates = jax.nn.softmax(top_v, axis=-1)
    combine = jnp.sum(jax.nn.one_hot(top_i, N_EXPERTS, dtype=jnp.float32) * gates[..., None], axis=1)
    combine = combine.astype(xt.dtype)
    out = jnp.zeros_like(xt)
    for e in range(N_EXPERTS):
        out = out + combine[:, e:e + 1] * swiglu(xt, eg[e], eu[e], ed[e])
    return out.reshape(bsz, seq, d)


def setup_inputs(seed: int = 0) -> dict:
    key = jax.random.key(seed)
    ks = jax.random.split(key, 32)
    f32 = jnp.float32
    nrm = lambda k, shape, scale: jax.random.normal(k, shape, f32) * scale
    gain = lambda k, shape: 1.0 + 0.02 * jax.random.normal(k, shape, f32)
    return {
        "x": nrm(ks[0], (BATCH, SEQ, D_MODEL), 1.0),
        "p": nrm(ks[1], (DEPTH, BATCH, SEQ, PLE_DIM), 1.0),
        "ln_mix_g": gain(ks[2], (DEPTH, D_MODEL)),
        "w_in": nrm(ks[3], (DEPTH, D_MODEL, N_IN), D_MODEL ** -0.5),
        "w_a_up": nrm(ks[4], (DEPTH, GATE_RANK, GLA_KEY), GATE_RANK ** -0.5),
        "b_a": nrm(ks[5], (DEPTH, GLA_KEY), 0.1),
        "gla_g": gain(ks[6], (DEPTH, GLA_VAL)),
        "conv_w": nrm(ks[7], (DEPTH, CONV_K, CONV_WIDTH), CONV_K ** -0.5),
        "conv_g": gain(ks[8], (DEPTH, CONV_WIDTH)),
        "w_out": nrm(ks[9], (DEPTH, MIX_WIDTH, D_MODEL), MIX_WIDTH ** -0.5),
        "ln_ffn_g": gain(ks[10], (DEPTH, D_MODEL)),
        "ffn_wg": nrm(ks[11], (N_DENSE, D_MODEL, D_FF), D_MODEL ** -0.5),
        "ffn_wu": nrm(ks[12], (N_DENSE, D_MODEL, D_FF), D_MODEL ** -0.5),
        "ffn_wd": nrm(ks[13], (N_DENSE, D_FF, D_MODEL), D_FF ** -0.5),
        "router_w": nrm(ks[14], (N_MOE, D_MODEL, N_EXPERTS), D_MODEL ** -0.5),
        "router_b": nrm(ks[15], (N_MOE, N_EXPERTS), 0.01),
        "exp_wg": nrm(ks[16], (N_MOE, N_EXPERTS, D_MODEL, D_EXPERT), D_MODEL ** -0.5),
        "exp_wu": nrm(ks[17], (N_MOE, N_EXPERTS, D_MODEL, D_EXPERT), D_MODEL ** -0.5),
        "exp_wd": nrm(ks[18], (N_MOE, N_EXPERTS, D_EXPERT, D_MODEL), D_EXPERT ** -0.5),
        "ln_ple_g": gain(ks[19], (DEPTH, D_MODEL)),
        "ple_gd": nrm(ks[20], (DEPTH, D_MODEL, PLE_DIM), D_MODEL ** -0.5),
        "ple_gu": nrm(ks[21], (DEPTH, PLE_DIM, D_MODEL), PLE_DIM ** -0.5),
        "ple_proj": nrm(ks[22], (DEPTH, PLE_DIM, D_MODEL), PLE_DIM ** -0.5),
        "final_g": gain(ks[23], (D_MODEL,)),
    }


def reference(x, p, ln_mix_g, w_in, w_a_up, b_a, gla_g, conv_w, conv_g, w_out,
              ln_ffn_g, ffn_wg, ffn_wu, ffn_wd, router_w, router_b,
              exp_wg, exp_wu, exp_wd, ln_ple_g, ple_gd, ple_gu, ple_proj, final_g):
    h = x
    for i in range(DEPTH):
        xn = rmsnorm(h, ln_mix_g[i])
        h = h + hybrid_mixer(xn, w_in[i], w_a_up[i], b_a[i], gla_g[i],
                             conv_w[i], conv_g[i], w_out[i])
        hn = rmsnorm(h, ln_ffn_g[i])
        j = i // 2
        if i % 2 == 0:
            h = h + swiglu(hn, ffn_wg[j], ffn_wu[j], ffn_wd[j])
        else:
            h = h + moe_swiglu(hn, router_w[j], router_b[j], exp_wg[j], exp_wu[j], exp_wd[j])
        gn = rmsnorm(h, ln_ple_g[i])
        gate = jax.nn.sigmoid((gn @ ple_gd[i]) @ ple_gu[i])
        h = h + gate * (p[i].astype(h.dtype) @ ple_proj[i])
    return rmsnorm(h, final_g)
```

```python
import functools

import jax
import jax.numpy as jnp
from jax import lax
from jax.experimental import pallas as pl
from jax.experimental.pallas import tpu as pltpu

F32 = jnp.float32
BF16 = jnp.bfloat16
HIGHEST = lax.Precision.HIGHEST

EPS = 1e-6
LANES = 128
GLA_HEADS = 8
GLA_DK = 128
GLA_DV = 256
GATE_RANK = 16
GATE_TAU = 16.0
CHUNK = 64
SUB = 16
N_SUB = CHUNK // SUB
CONV_GROUPS = 16
CONV_GROUP_W = 128
N_EXPERTS = 8
NEG_BIG = -1e30
VMEM_LIMIT = 56 * 1024 * 1024


def _cparams(*sem):
    return pltpu.CompilerParams(dimension_semantics=sem, vmem_limit_bytes=VMEM_LIMIT)


def _rms(x, g):
    ms = jnp.mean(x * x, axis=-1, keepdims=True)
    return x * lax.rsqrt(ms + EPS) * g


def _sigmoid(x):
    return 1.0 / (1.0 + jnp.exp(-x))


def _norm_a_body(h_ref, g_ref, wa_ref, xn_ref, a_ref):
    xn = _rms(h_ref[...], g_ref[...]).astype(BF16)
    xn_ref[...] = xn
    a_ref[...] = jnp.dot(xn, wa_ref[...], preferred_element_type=F32)


def norm_a(h, g, wa, tm=256):
    m, d = h.shape
    tm = min(tm, m)
    return pl.pallas_call(
        _norm_a_body,
        grid=(m // tm,),
        in_specs=[pl.BlockSpec((tm, d), lambda i: (i, 0)),
                  pl.BlockSpec((1, d), lambda i: (0, 0)),
                  pl.BlockSpec((d, LANES), lambda i: (0, 0))],
        out_specs=[pl.BlockSpec((tm, d), lambda i: (i, 0)),
                   pl.BlockSpec((tm, LANES), lambda i: (i, 0))],
        out_shape=[jax.ShapeDtypeStruct((m, d), BF16),
                   jax.ShapeDtypeStruct((m, LANES), F32)],
        compiler_params=_cparams("parallel"),
        name="norm_a",
    )(h, g, wa)


def _proj_body(x_ref, w_ref, o_ref):
    o_ref[...] = jnp.dot(x_ref[...], w_ref[...],
                         preferred_element_type=F32).astype(o_ref.dtype)


def proj(x, w, tm=1024, tn=1024):
    m, d = x.shape
    n = w.shape[1]
    tm = min(tm, m)
    return pl.pallas_call(
        _proj_body,
        grid=(m // tm, n // tn),
        in_specs=[pl.BlockSpec((tm, d), lambda i, j: (i, 0)),
                  pl.BlockSpec((d, tn), lambda i, j: (0, j))],
        out_specs=pl.BlockSpec((tm, tn), lambda i, j: (i, j)),
        out_shape=jax.ShapeDtypeStruct((m, n), BF16),
        compiler_params=_cparams("parallel", "arbitrary"),
        name="in_proj",
    )(x, w)


def _gla_body(q_ref, k_ref, v_ref, r_ref, a_ref, wup_ref, ba_ref, g_ref,
              y_ref, st_ref, ks_ref, bs_ref, *, n_chunks):
    @pl.when(pl.program_id(2) == 0)
    def _():
        st_ref[...] = jnp.zeros_like(st_ref)

    row = lax.broadcasted_iota(jnp.int32, (CHUNK, CHUNK), 0)
    col = lax.broadcasted_iota(jnp.int32, (CHUNK, CHUNK), 1)
    tril = (row >= col).astype(F32)
    dloc = col - (row // SUB) * SUB
    dsel = jnp.where((dloc >= 0) & (dloc <= row % SUB), dloc, -1)
    krow = lax.broadcasted_iota(jnp.int32, (CHUNK, GLA_DK), 0)
    scale = GLA_DK ** -0.5

    def chunk(c, carry):
        r0 = pl.multiple_of(c * CHUNK, CHUNK)
        rows = pl.ds(r0, CHUNK)
        qc = q_ref[rows, :].astype(F32) * scale
        kc = k_ref[rows, :].astype(F32)
        vc = v_ref[rows, :]
        z = jnp.dot(a_ref[rows, :], wup_ref[...], precision=HIGHEST,
                    preferred_element_type=F32) + ba_ref[...]
        log_a = (jnp.minimum(z, 0.0) - jnp.log1p(jnp.exp(-jnp.abs(z)))) * (1.0 / GATE_TAU)
        bc = jnp.dot(tril, log_a, precision=HIGHEST, preferred_element_type=F32)
        ks_ref[...] = kc
        bs_ref[...] = bc

        amat = jnp.zeros((CHUNK, CHUNK), F32)
        for j in range(SUB):
            kj = jnp.concatenate(
                [jnp.broadcast_to(ks_ref[pl.ds(i * SUB + j, 1), :], (SUB, GLA_DK))
                 for i in range(N_SUB)], axis=0)
            bj = jnp.concatenate(
                [jnp.broadcast_to(bs_ref[pl.ds(i * SUB + j, 1), :], (SUB, GLA_DK))
                 for i in range(N_SUB)], axis=0)
            pj = qc * jnp.exp(bc - bj) * kj
            amat = jnp.where(dsel == j, jnp.sum(pj, axis=-1, keepdims=True), amat)

        off_rows = [jnp.zeros((SUB, CHUNK), F32)]
        for i in range(1, N_SUB):
            ref_i = bs_ref[pl.ds(i * SUB, 1), :]
            qs = qc[i * SUB:(i + 1) * SUB] * jnp.exp(bc[i * SUB:(i + 1) * SUB] - ref_i)
            ks = jnp.where(krow < i * SUB, kc * jnp.exp(ref_i - bc), 0.0)
            off_rows.append(lax.dot_general(
                qs.astype(BF16), ks.astype(BF16), (((1,), (1,)), ((), ())),
                preferred_element_type=F32))
        amat = amat + jnp.concatenate(off_rows, axis=0)

        st = st_ref[...]
        qb = (qc * jnp.exp(bc)).astype(BF16)
        o = jnp.dot(amat.astype(BF16), vc, preferred_element_type=F32)
        o = o + lax.dot_general(qb, st.astype(BF16), (((1,), (1,)), ((), ())),
                                preferred_element_type=F32)
        b_last = bs_ref[pl.ds(CHUNK - 1, 1), :]
        kd = (kc * jnp.exp(b_last - bc)).astype(BF16)
        st_ref[...] = st * jnp.exp(b_last) + lax.dot_general(
            vc, kd, (((0,), (0,)), ((), ())), preferred_element_type=F32)

        rg = r_ref[rows, :].astype(F32)
        y = _rms(o, g_ref[...]) * (rg * _sigmoid(rg))
        y_ref[rows, :] = y.astype(y_ref.dtype)
        return carry

    lax.fori_loop(0, n_chunks, chunk, 0)


def gla(pj, a, wup, ba, g, batch, seq, tb=512):
    m = pj.shape[0]
    tb = min(tb, seq)
    nt = seq // tb
    kq = GLA_HEADS
    vq = (2 * GLA_HEADS * GLA_DK) // GLA_DV
    rq = vq + GLA_HEADS
    tok = lambda b, h, t: b * nt + t
    return pl.pallas_call(
        functools.partial(_gla_body, n_chunks=tb // CHUNK),
        grid=(batch, GLA_HEADS, nt),
        in_specs=[
            pl.BlockSpec((tb, GLA_DK), lambda b, h, t: (tok(b, h, t), h)),
            pl.BlockSpec((tb, GLA_DK), lambda b, h, t: (tok(b, h, t), kq + h)),
            pl.BlockSpec((tb, GLA_DV), lambda b, h, t: (tok(b, h, t), vq + h)),
            pl.BlockSpec((tb, GLA_DV), lambda b, h, t: (tok(b, h, t), rq + h)),
            pl.BlockSpec((tb, LANES), lambda b, h, t: (tok(b, h, t), 0)),
            pl.BlockSpec((LANES, GLA_DK), lambda b, h, t: (0, h)),
            pl.BlockSpec((1, GLA_DK), lambda b, h, t: (0, h)),
            pl.BlockSpec((1, GLA_DV), lambda b, h, t: (0, h)),
        ],
        out_specs=pl.BlockSpec((tb, GLA_DV), lambda b, h, t: (tok(b, h, t), h)),
        out_shape=jax.ShapeDtypeStruct((m, GLA_HEADS * GLA_DV), BF16),
        scratch_shapes=[pltpu.VMEM((GLA_DV, GLA_DK), F32),
                        pltpu.VMEM((CHUNK, GLA_DK), F32),
                        pltpu.VMEM((CHUNK, GLA_DK), F32)],
        compiler_params=_cparams("parallel", "parallel", "arbitrary"),
        name="gla",
    )(pj, pj, pj, pj, a, wup, ba, g)


def _conv_body(b_ref, c_ref, u_ref, w_ref, g_ref, y_ref, tail_ref):
    @pl.when(pl.program_id(1) == 0)
    def _():
        tail_ref[...] = jnp.zeros_like(tail_ref)

    tb = b_ref.shape[0]
    row = lax.broadcasted_iota(jnp.int32, (tb, CONV_GROUP_W), 0)
    for gi in range(CONV_GROUPS):
        cols = slice(gi * CONV_GROUP_W, (gi + 1) * CONV_GROUP_W)
        cu = c_ref[:, cols].astype(F32) * u_ref[:, cols].astype(F32)
        t1 = tail_ref[pl.ds(7, 1), cols]
        t2 = tail_ref[pl.ds(6, 1), cols]
        cu1 = jnp.where(row == 0, t1, pltpu.roll(cu, 1, axis=0))
        cu2 = jnp.where(row == 0, t2, jnp.where(row == 1, t1, pltpu.roll(cu, 2, axis=0)))
        conv = (w_ref[pl.ds(0, 1), cols] * cu2 + w_ref[pl.ds(1, 1), cols] * cu1
                + w_ref[pl.ds(2, 1), cols] * cu)
        yb = b_ref[:, cols].astype(F32) * conv
        y_ref[:, cols] = _rms(yb, g_ref[:, cols]).astype(y_ref.dtype)
        tail_ref[:, cols] = cu[tb - 8:, :]


def sconv(pj, w, g, batch, seq, tb=256):
    m = pj.shape[0]
    cw = CONV_GROUPS * CONV_GROUP_W
    tb = min(tb, seq)
    nt = seq // tb
    first = (2 * GLA_HEADS * GLA_DK + 2 * GLA_HEADS * GLA_DV) // cw
    spec = lambda k: pl.BlockSpec((tb, cw), lambda b, t: (b * nt + t, first + k))
    return pl.pallas_call(
        _conv_body,
        grid=(batch, nt),
        in_specs=[spec(0), spec(1), spec(2),
                  pl.BlockSpec((8, cw), lambda b, t: (0, 0)),
                  pl.BlockSpec((1, cw), lambda b, t: (0, 0))],
        out_specs=pl.BlockSpec((tb, cw), lambda b, t: (b * nt + t, 0)),
        out_shape=jax.ShapeDtypeStruct((m, cw), BF16),
        scratch_shapes=[pltpu.VMEM((8, cw), F32)],
        compiler_params=_cparams("parallel", "arbitrary"),
        name="sconv",
    )(pj, pj, pj, w, g)


def _out_body(ya_ref, yb_ref, wa_ref, wb_ref, h_ref, o_ref):
    acc = jnp.dot(ya_ref[...], wa_ref[...], preferred_element_type=F32)
    acc = acc + jnp.dot(yb_ref[...], wb_ref[...], preferred_element_type=F32)
    o_ref[...] = h_ref[...] + acc


def out_proj(ya, yb, w, h, tm=1024, tn=512):
    m, ka = ya.shape
    kb = yb.shape[1]
    n = w.shape[1]
    tm = min(tm, m)
    nka = ka // kb
    return pl.pallas_call(
        _out_body,
        grid=(m // tm, n // tn),
        in_specs=[pl.BlockSpec((tm, ka), lambda i, j: (i, 0)),
                  pl.BlockSpec((tm, kb), lambda i, j: (i, 0)),
                  pl.BlockSpec((ka, tn), lambda i, j: (0, j)),
                  pl.BlockSpec((kb, tn), lambda i, j: (nka, j)),
                  pl.BlockSpec((tm, tn), lambda i, j: (i, j))],
        out_specs=pl.BlockSpec((tm, tn), lambda i, j: (i, j)),
        out_shape=jax.ShapeDtypeStruct((m, n), F32),
        compiler_params=_cparams("parallel", "arbitrary"),
        name="out_proj",
    )(ya, yb, w, w, h)


def _ffn_body(h_ref, g_ref, wg_ref, wu_ref, wd_ref, o_ref, xn_ref):
    @pl.when(pl.program_id(1) == 0)
    def _():
        x = h_ref[...]
        xn_ref[...] = _rms(x, g_ref[...]).astype(BF16)
        o_ref[...] = x

    xn = xn_ref[...]
    gg = jnp.dot(xn, wg_ref[...], preferred_element_type=F32)
    uu = jnp.dot(xn, wu_ref[...], preferred_element_type=F32)
    act = (gg * _sigmoid(gg) * uu).astype(BF16)
    o_ref[...] += jnp.dot(act, wd_ref[...], preferred_element_type=F32)


def ffn(h, g, wg, wu, wd, tm=512, tf=256):
    m, d = h.shape
    f = wg.shape[1]
    tm = min(tm, m)
    return pl.pallas_call(
        _ffn_body,
        grid=(m // tm, f // tf),
        in_specs=[pl.BlockSpec((tm, d), lambda i, j: (i, 0), pipeline_mode=pl.Buffered(1)),
                  pl.BlockSpec((1, d), lambda i, j: (0, 0)),
                  pl.BlockSpec((d, tf), lambda i, j: (0, j)),
                  pl.BlockSpec((d, tf), lambda i, j: (0, j)),
                  pl.BlockSpec((tf, d), lambda i, j: (j, 0))],
        out_specs=pl.BlockSpec((tm, d), lambda i, j: (i, 0)),
        out_shape=jax.ShapeDtypeStruct((m, d), F32),
        scratch_shapes=[pltpu.VMEM((tm, d), BF16)],
        compiler_params=_cparams("parallel", "arbitrary"),
        name="ffn",
    )(h, g, wg, wu, wd)


def _router_body(h_ref, g_ref, rw_ref, rb_ref, info_ref, gate_ref, cnt_ref, carry_ref):
    @pl.when(pl.program_id(0) == 0)
    def _():
        carry_ref[...] = jnp.zeros_like(carry_ref)

    tm = h_ref.shape[0]
    hn = _rms(h_ref[...], g_ref[...])
    logits = jnp.dot(hn, rw_ref[...], precision=HIGHEST,
                     preferred_element_type=F32) + rb_ref[...]
    col = lax.broadcasted_iota(jnp.int32, (tm, LANES), 1)
    m1 = jnp.max(logits, axis=-1, keepdims=True)
    i1 = jnp.min(jnp.where(logits == m1, col, LANES), axis=-1, keepdims=True)
    rest = jnp.where(col == i1, 2.0 * NEG_BIG, logits)
    m2 = jnp.max(rest, axis=-1, keepdims=True)
    i2 = jnp.min(jnp.where(rest == m2, col, LANES), axis=-1, keepdims=True)
    e21 = jnp.exp(m2 - m1)
    g1 = 1.0 / (1.0 + e21)
    g2 = e21 / (1.0 + e21)

    sel1 = col == i1
    sel2 = col == i2
    onehot = jnp.where(sel1, 1.0, jnp.where(sel2, 1.0, 0.0))
    r = lax.broadcasted_iota(jnp.int32, (tm, tm), 0)
    c = lax.broadcasted_iota(jnp.int32, (tm, tm), 1)
    strict = jnp.where(r > c, 1.0, 0.0).astype(BF16)
    before = jnp.dot(strict, onehot.astype(BF16), preferred_element_type=F32) + carry_ref[...]
    rank1 = jnp.sum(jnp.where(sel1, before, 0.0), axis=-1, keepdims=True).astype(jnp.int32)
    rank2 = jnp.sum(jnp.where(sel2, before, 0.0), axis=-1, keepdims=True).astype(jnp.int32)
    carry_ref[...] = carry_ref[...] + jnp.sum(onehot, axis=0, keepdims=True)

    info_ref[...] = jnp.where(col == 0, i1, jnp.where(col == 1, i2,
                              jnp.where(col == 2, rank1, jnp.where(col == 3, rank2, 0))))
    gate_ref[...] = jnp.where(col == 0, g1, jnp.where(col == 1, g2, 0.0))
    cnt_ref[...] = carry_ref[...]


def router(h, g, rw, rb, tm=256):
    m, d = h.shape
    tm = min(tm, m)
    return pl.pallas_call(
        _router_body,
        grid=(m // tm,),
        in_specs=[pl.BlockSpec((tm, d), lambda i: (i, 0)),
                  pl.BlockSpec((1, d), lambda i: (0, 0)),
                  pl.BlockSpec((d, LANES), lambda i: (0, 0)),
                  pl.BlockSpec((1, LANES), lambda i: (0, 0))],
        out_specs=[pl.BlockSpec((tm, LANES), lambda i: (i, 0)),
                   pl.BlockSpec((tm, LANES), lambda i: (i, 0)),
                   pl.BlockSpec((1, LANES), lambda i: (0, 0))],
        out_shape=[jax.ShapeDtypeStruct((m, LANES), jnp.int32),
                   jax.ShapeDtypeStruct((m, LANES), F32),
                   jax.ShapeDtypeStruct((1, LANES), F32)],
        scratch_shapes=[pltpu.VMEM((1, LANES), F32)],
        compiler_params=_cparams("arbitrary"),
        name="router",
    )(h, g, rw, rb)


def _row_copy(src_hbm, dst_vmem, sem, tok, r):
    return pltpu.make_async_copy(src_hbm.at[pl.ds(tok, 1), :],
                                 dst_vmem.at[pl.ds(r, 1), :], sem)


def _gather_body(idx_ref, x_hbm, o_ref, sem):
    tg = o_ref.shape[0]

    def issue(r, carry):
        _row_copy(x_hbm, o_ref, sem, idx_ref[0, 0, r], r).start()
        return carry

    lax.fori_loop(0, tg, issue, 0, unroll=8)
    pltpu.make_async_copy(x_hbm.at[pl.ds(0, tg), :], o_ref, sem).wait()


def gather_rows(x, idx, tg=256):
    p = idx.shape[0]
    d = x.shape[1]
    tg = min(tg, p)
    return pl.pallas_call(
        _gather_body,
        grid=(p // tg,),
        in_specs=[pl.BlockSpec((1, 1, tg), lambda i: (i, 0, 0), memory_space=pltpu.SMEM),
                  pl.BlockSpec(memory_space=pl.ANY)],
        out_specs=pl.BlockSpec((tg, d), lambda i: (i, 0)),
        out_shape=jax.ShapeDtypeStruct((p, d), x.dtype),
        scratch_shapes=[pltpu.SemaphoreType.DMA(())],
        compiler_params=_cparams("arbitrary"),
        name="moe_gather",
    )(idx.reshape(p // tg, 1, tg), x)


def _expert_body(te_ref, nv_ref, x_ref, g_ref, gs_ref, wg_ref, wu_ref, wd_ref,
                 o_ref, xn_ref):
    i = pl.program_id(0)
    j = pl.program_id(1)
    valid = i < nv_ref[0]

    @pl.when(j == 0)
    def _():
        xn_ref[...] = _rms(x_ref[...], g_ref[...]).astype(BF16)
        o_ref[...] = jnp.zeros_like(o_ref)

    @pl.when(valid)
    def _():
        xn = xn_ref[...]
        gg = jnp.dot(xn, wg_ref[...], preferred_element_type=F32)
        uu = jnp.dot(xn, wu_ref[...], preferred_element_type=F32)
        act = (gg * _sigmoid(gg) * uu).astype(BF16)
        o_ref[...] += jnp.dot(act, wd_ref[...], preferred_element_type=F32)

    @pl.when(j == pl.num_programs(1) - 1)
    def _():
        o_ref[...] = o_ref[...] * gs_ref[...]


def expert_ffn(xs, g, gs, tile_expert, n_valid, wg, wu, wd, tm, tf=256):
    p, d = xs.shape
    f = wg.shape[2]
    nf = f // tf

    def fblk(i, j, te, nv):
        return jnp.where(i < nv[0], j, nf - 1)

    grid_spec = pltpu.PrefetchScalarGridSpec(
        num_scalar_prefetch=2,
        grid=(p // tm, nf),
        in_specs=[pl.BlockSpec((tm, d), lambda i, j, te, nv: (i, 0), pipeline_mode=pl.Buffered(1)),
                  pl.BlockSpec((1, d), lambda i, j, te, nv: (0, 0)),
                  pl.BlockSpec((tm, 1), lambda i, j, te, nv: (i, 0)),
                  pl.BlockSpec((None, d, tf), lambda i, j, te, nv: (te[i], 0, fblk(i, j, te, nv))),
                  pl.BlockSpec((None, d, tf), lambda i, j, te, nv: (te[i], 0, fblk(i, j, te, nv))),
                  pl.BlockSpec((None, tf, d), lambda i, j, te, nv: (te[i], fblk(i, j, te, nv), 0))],
        out_specs=pl.BlockSpec((tm, d), lambda i, j, te, nv: (i, 0)),
        scratch_shapes=[pltpu.VMEM((tm, d), BF16)],
    )
    return pl.pallas_call(
        _expert_body,
        grid_spec=grid_spec,
        out_shape=jax.ShapeDtypeStruct((p, d), F32),
        compiler_params=_cparams("arbitrary", "arbitrary"),
        name="moe_experts",
    )(tile_expert, n_valid, xs, g, gs, wg, wu, wd)


def _combine_body(p1_ref, p2_ref, h_ref, ys_hbm, o_ref, a_ref, b_ref, sem):
    tc = o_ref.shape[0]

    def issue(r, carry):
        _row_copy(ys_hbm, a_ref, sem.at[0], p1_ref[0, 0, r], r).start()
        _row_copy(ys_hbm, b_ref, sem.at[1], p2_ref[0, 0, r], r).start()
        return carry

    lax.fori_loop(0, tc, issue, 0, unroll=8)
    pltpu.make_async_copy(ys_hbm.at[pl.ds(0, tc), :], a_ref, sem.at[0]).wait()
    pltpu.make_async_copy(ys_hbm.at[pl.ds(0, tc), :], b_ref, sem.at[1]).wait()
    o_ref[...] = h_ref[...] + a_ref[...] + b_ref[...]


def combine(h, ys, pos1, pos2, tc=256):
    m, d = h.shape
    tc = min(tc, m)
    ispec = pl.BlockSpec((1, 1, tc), lambda i: (i, 0, 0), memory_space=pltpu.SMEM)
    return pl.pallas_call(
        _combine_body,
        grid=(m // tc,),
        in_specs=[ispec, ispec,
                  pl.BlockSpec((tc, d), lambda i: (i, 0)),
                  pl.BlockSpec(memory_space=pl.ANY)],
        out_specs=pl.BlockSpec((tc, d), lambda i: (i, 0)),
        out_shape=jax.ShapeDtypeStruct((m, d), F32),
        scratch_shapes=[pltpu.VMEM((tc, d), F32), pltpu.VMEM((tc, d), F32),
                        pltpu.SemaphoreType.DMA((2,))],
        compiler_params=_cparams("arbitrary"),
        name="moe_combine",
    )(pos1.reshape(m // tc, 1, tc), pos2.reshape(m // tc, 1, tc), h, ys)


def moe(h, g, rw, rb, wg, wu, wd, tm=512):
    m, d = h.shape
    tm = min(tm, m)
    info, gates, counts = router(h, g, rw, rb)
    e1, e2, r1, r2 = info[:, 0], info[:, 1], info[:, 2], info[:, 3]
    g1, g2 = gates[:, 0], gates[:, 1]
    counts = counts[0, :N_EXPERTS].astype(jnp.int32)

    tiles_e = (counts + tm - 1) // tm
    tile_end = jnp.cumsum(tiles_e)
    row_off = (tile_end - tiles_e) * tm
    n_tiles = (2 * m) // tm + N_EXPERTS
    p = n_tiles * tm
    pos1 = row_off[e1] + r1
    pos2 = row_off[e2] + r2
    pos = jnp.concatenate([pos1, pos2])
    tok = jnp.arange(m, dtype=jnp.int32)
    src = jnp.zeros((p,), jnp.int32).at[pos].set(jnp.concatenate([tok, tok]))
    gs = jnp.zeros((p,), F32).at[pos].set(jnp.concatenate([g1, g2]))
    n_valid = tile_end[-1:].astype(jnp.int32)
    tile_id = jnp.minimum(jnp.arange(n_tiles, dtype=jnp.int32), n_valid[0] - 1)
    tile_expert = jnp.searchsorted(tile_end, tile_id, side="right").astype(jnp.int32)

    xs = gather_rows(h, src)
    ys = expert_ffn(xs, g, gs.reshape(p, 1), tile_expert, n_valid, wg, wu, wd, tm)
    return combine(h, ys, pos1, pos2)


def _ple_core(h_ref, g_ref, gd_ref, gu_ref, p_ref, pw_ref):
    h = h_ref[...]
    gn = _rms(h, g_ref[...]).astype(BF16)
    t = jnp.dot(gn, gd_ref[...], preferred_element_type=F32).astype(BF16)
    gate = _sigmoid(jnp.dot(t, gu_ref[...], preferred_element_type=F32))
    emb = jnp.dot(p_ref[...].astype(BF16), pw_ref[...], preferred_element_type=F32)
    return h + gate * emb


def _ple_next_body(h_ref, g_ref, gd_ref, gu_ref, p_ref, pw_ref, gn_ref, wa_ref,
                   o_ref, xn_ref, a_ref):
    hn = _ple_core(h_ref, g_ref, gd_ref, gu_ref, p_ref, pw_ref)
    o_ref[...] = hn
    xn = _rms(hn, gn_ref[...]).astype(BF16)
    xn_ref[...] = xn
    a_ref[...] = jnp.dot(xn, wa_ref[...], preferred_element_type=F32)


def _ple_final_body(h_ref, g_ref, gd_ref, gu_ref, p_ref, pw_ref, gn_ref, o_ref):
    hn = _ple_core(h_ref, g_ref, gd_ref, gu_ref, p_ref, pw_ref)
    o_ref[...] = _rms(hn, gn_ref[...])


def ple(h, g, gd, gu, p, pw, g_next, wa_next=None, tm=256):
    m, d = h.shape
    e = p.shape[1]
    tm = min(tm, m)
    row = lambda w: pl.BlockSpec((tm, w), lambda i: (i, 0))
    full = lambda a, b: pl.BlockSpec((a, b), lambda i: (0, 0))
    in_specs = [row(d), full(1, d), full(d, e), full(e, d), row(e), full(e, d), full(1, d)]
    if wa_next is None:
        return pl.pallas_call(
            _ple_final_body, grid=(m // tm,), in_specs=in_specs, out_specs=row(d),
            out_shape=jax.ShapeDtypeStruct((m, d), F32),
            compiler_params=_cparams("parallel"), name="ple_final",
        )(h, g, gd, gu, p, pw, g_next)
    return pl.pallas_call(
        _ple_next_body, grid=(m // tm,), in_specs=in_specs + [full(d, LANES)],
        out_specs=[row(d), row(d), row(LANES)],
        out_shape=[jax.ShapeDtypeStruct((m, d), F32),
                   jax.ShapeDtypeStruct((m, d), BF16),
                   jax.ShapeDtypeStruct((m, LANES), F32)],
        compiler_params=_cparams("parallel"), name="ple_next",
    )(h, g, gd, gu, p, pw, g_next, wa_next)


def kernel(x, p, ln_mix_g, w_in, w_a_up, b_a, gla_g, conv_w, conv_g, w_out, ln_ffn_g, ffn_wg, ffn_wu, ffn_wd, router_w, router_b, exp_wg, exp_wu, exp_wd, ln_ple_g, ple_gd, ple_gu, ple_proj, final_g):
    batch, seq, d = x.shape
    depth = w_in.shape[0]
    m = batch * seq
    a0 = 2 * GLA_HEADS * GLA_DK + 2 * GLA_HEADS * GLA_DV
    row2 = lambda v: v.reshape(1, -1)

    def gate_in_w(i):
        return jnp.pad(w_in[i][:, a0:a0 + GATE_RANK], ((0, 0), (0, LANES - GATE_RANK))).astype(BF16)

    h = x.reshape(m, d)
    xn, a = norm_a(h, row2(ln_mix_g[0]), gate_in_w(0))
    out = None
    for i in range(depth):
        w_main = jnp.concatenate([w_in[i][:, :a0], w_in[i][:, a0 + GATE_RANK:]], axis=1).astype(BF16)
        pj = proj(xn, w_main)
        wup = jnp.pad(w_a_up[i], ((0, LANES - GATE_RANK), (0, 0)))
        ya = gla(pj, a, wup, row2(b_a[i]), row2(gla_g[i]), batch, seq)
        cw = jnp.pad(conv_w[i], ((0, 8 - conv_w.shape[1]), (0, 0)))
        yb = sconv(pj, cw, row2(conv_g[i]), batch, seq)
        h = out_proj(ya, yb, w_out[i].astype(BF16), h)

        j = i // 2
        if i % 2 == 0:
            h = ffn(h, row2(ln_ffn_g[i]), ffn_wg[j].astype(BF16), ffn_wu[j].astype(BF16),
                    ffn_wd[j].astype(BF16))
        else:
            rw = jnp.pad(router_w[j], ((0, 0), (0, LANES - N_EXPERTS)))
            rb = jnp.pad(router_b[j], (0, LANES - N_EXPERTS), constant_values=NEG_BIG)
            h = moe(h, row2(ln_ffn_g[i]), rw, row2(rb), exp_wg[j].astype(BF16),
                    exp_wu[j].astype(BF16), exp_wd[j].astype(BF16))

        ple_args = (h, row2(ln_ple_g[i]), ple_gd[i].astype(BF16), ple_gu[i].astype(BF16),
                    p[i].reshape(m, -1), ple_proj[i].astype(BF16))
        if i + 1 < depth:
            h, xn, a = ple(*ple_args, row2(ln_mix_g[i + 1]), gate_in_w(i + 1))
        else:
            out = ple(*ple_args, row2(final_g))
    return out.reshape(batch, seq, d)
```

```python
import functools

import jax
import jax.numpy as jnp
from jax import lax
from jax.experimental import pallas as pl
from jax.experimental.pallas import tpu as pltpu

F32 = jnp.float32
BF16 = jnp.bfloat16
HIGHEST = lax.Precision.HIGHEST

EPS = 1e-6
LANES = 128
GLA_HEADS = 8
GLA_DK = 128
GLA_DV = 256
GATE_RANK = 16
GATE_TAU = 16.0
CHUNK = 64
SUB = 16
N_SUB = CHUNK // SUB
CONV_GROUPS = 16
CONV_GROUP_W = 128
N_EXPERTS = 8
LOG2E = 1.4426950408889634
ROW_SLAB = 128
DOWN_SLAB = 512
FFN_TF = 512
NEG_BIG = -1e30
VMEM_LIMIT = 56 * 1024 * 1024


def _cparams(*sem):
    return pltpu.CompilerParams(dimension_semantics=sem, vmem_limit_bytes=VMEM_LIMIT)


def _rms(x, g):
    ms = jnp.mean(x * x, axis=-1, keepdims=True)
    return x * lax.rsqrt(ms + EPS) * g


def _sigmoid(x):
    return 1.0 / (1.0 + jnp.exp(-x))


def _for_row_slabs(ref, fn):
    n = ref.shape[0]
    slab = min(ROW_SLAB, n)

    def body(c, carry):
        fn(pl.ds(pl.multiple_of(c * slab, slab), slab))
        return carry

    lax.fori_loop(0, n // slab, body, 0)


def _rms_rows(src_ref, g_ref, dst_ref):
    def slab(rows):
        dst_ref[rows, :] = _rms(src_ref[rows, :], g_ref[...]).astype(dst_ref.dtype)

    _for_row_slabs(src_ref, slab)


def _norm_a_body(h_ref, g_ref, wa_ref, xn_ref, a_ref):
    xn = _rms(h_ref[...], g_ref[...]).astype(BF16)
    xn_ref[...] = xn
    a_ref[...] = jnp.dot(xn, wa_ref[...], preferred_element_type=F32)


def norm_a(h, g, wa, tm=256):
    m, d = h.shape
    tm = min(tm, m)
    return pl.pallas_call(
        _norm_a_body,
        grid=(m // tm,),
        in_specs=[pl.BlockSpec((tm, d), lambda i: (i, 0)),
                  pl.BlockSpec((1, d), lambda i: (0, 0)),
                  pl.BlockSpec((d, LANES), lambda i: (0, 0))],
        out_specs=[pl.BlockSpec((tm, d), lambda i: (i, 0)),
                   pl.BlockSpec((tm, LANES), lambda i: (i, 0))],
        out_shape=[jax.ShapeDtypeStruct((m, d), BF16),
                   jax.ShapeDtypeStruct((m, LANES), F32)],
        compiler_params=_cparams("parallel"),
        name="norm_a",
    )(h, g, wa)


def _proj_body(x_ref, w_ref, o_ref):
    o_ref[...] = jnp.dot(x_ref[...], w_ref[...],
                         preferred_element_type=F32).astype(o_ref.dtype)


def proj(x, w, tm=1024, tn=1024):
    m, d = x.shape
    n = w.shape[1]
    tm = min(tm, m)
    return pl.pallas_call(
        _proj_body,
        grid=(m // tm, n // tn),
        in_specs=[pl.BlockSpec((tm, d), lambda i, j: (i, 0)),
                  pl.BlockSpec((d, tn), lambda i, j: (0, j))],
        out_specs=pl.BlockSpec((tm, tn), lambda i, j: (i, j)),
        out_shape=jax.ShapeDtypeStruct((m, n), BF16),
        compiler_params=_cparams("parallel", "arbitrary"),
        name="in_proj",
    )(x, w)


def _gla_body(q_ref, k_ref, v_ref, r_ref, a_ref, wuh_ref, wul_ref, ba_ref, g_ref, y_ref,
              st_ref, bb_ref, qb_ref, u_ref, o_ref, ks_ref, bs_ref, *, n_chunks):
    @pl.when(pl.program_id(2) == 0)
    def _():
        st_ref[...] = jnp.zeros_like(st_ref)

    row = lax.broadcasted_iota(jnp.int32, (CHUNK, CHUNK), 0)
    col = lax.broadcasted_iota(jnp.int32, (CHUNK, CHUNK), 1)
    dloc = col - (row // SUB) * SUB
    dsel = jnp.where((dloc >= 0) & (dloc <= row % SUB), dloc, -1)
    krow = lax.broadcasted_iota(jnp.int32, (CHUNK, GLA_DK), 0)
    scale = GLA_DK ** -0.5

    a_hi = a_ref[...].astype(BF16)
    a_lo = (a_ref[...] - a_hi.astype(F32)).astype(BF16)
    z = (jnp.dot(a_hi, wuh_ref[...], preferred_element_type=F32)
         + jnp.dot(a_lo, wuh_ref[...], preferred_element_type=F32)
         + jnp.dot(a_hi, wul_ref[...], preferred_element_type=F32)) + ba_ref[...]
    csum = (jnp.minimum(z, 0.0) - jnp.log1p(jnp.exp(-jnp.abs(z)))) * (LOG2E / GATE_TAU)
    crow = lax.broadcasted_iota(jnp.int32, csum.shape, 0) % CHUNK
    shift = 1
    while shift < CHUNK:
        csum = csum + jnp.where(crow >= shift, pltpu.roll(csum, shift, axis=0), 0.0)
        shift *= 2
    bb_ref[...] = csum

    def intra(c, slot):
        r0 = pl.multiple_of(c * CHUNK, CHUNK)
        rows = pl.ds(r0, CHUNK)
        qc = q_ref[rows, :].astype(F32) * scale
        kc = k_ref[rows, :].astype(F32)
        vc = v_ref[rows, :]
        bc = bb_ref[rows, :]
        ks = ks_ref.at[slot]
        bs = bs_ref.at[slot]
        ks[...] = kc
        bs[...] = bc

        amat = jnp.zeros((CHUNK, CHUNK), F32)
        for j in range(SUB):
            kj = jnp.concatenate(
                [jnp.broadcast_to(ks[pl.ds(i * SUB + j, 1), :], (SUB, GLA_DK))
                 for i in range(N_SUB)], axis=0)
            bj = jnp.concatenate(
                [jnp.broadcast_to(bs[pl.ds(i * SUB + j, 1), :], (SUB, GLA_DK))
                 for i in range(N_SUB)], axis=0)
            pj = qc * jnp.exp2(bc - bj) * kj
            amat = jnp.where(dsel == j, jnp.sum(pj, axis=-1, keepdims=True), amat)

        off_rows = [jnp.zeros((SUB, CHUNK), F32)]
        for i in range(1, N_SUB):
            ref_i = bs[pl.ds(i * SUB, 1), :]
            qs = qc[i * SUB:(i + 1) * SUB] * jnp.exp2(bc[i * SUB:(i + 1) * SUB] - ref_i)
            kk = jnp.where(krow < i * SUB, kc * jnp.exp2(ref_i - bc), 0.0)
            off_rows.append(lax.dot_general(
                qs.astype(BF16), kk.astype(BF16), (((1,), (1,)), ((), ())),
                preferred_element_type=F32))
        amat = amat + jnp.concatenate(off_rows, axis=0)

        o_ref[rows, :] = jnp.dot(amat.astype(BF16), vc, preferred_element_type=F32)
        qb_ref[rows, :] = (qc * jnp.exp2(bc)).astype(BF16)
        b_last = bs[pl.ds(CHUNK - 1, 1), :]
        kd = (kc * jnp.exp2(b_last - bc)).astype(BF16)
        u_ref[c] = lax.dot_general(vc, kd, (((0,), (0,)), ((), ())),
                                   preferred_element_type=F32)

    def pair(i, carry):
        intra(2 * i, 0)
        intra(2 * i + 1, 1)
        return carry

    lax.fori_loop(0, n_chunks // 2, pair, 0)

    st = st_ref[...]
    for c in range(n_chunks):
        rows = slice(c * CHUNK, (c + 1) * CHUNK)
        o_ref[rows, :] += lax.dot_general(qb_ref[rows, :], st.astype(BF16),
                                          (((1,), (1,)), ((), ())),
                                          preferred_element_type=F32)
        st = st * jnp.exp2(bb_ref[(c + 1) * CHUNK - 1:(c + 1) * CHUNK, :]) + u_ref[c]
    st_ref[...] = st

    rg = r_ref[...].astype(F32)
    y_ref[...] = (_rms(o_ref[...], g_ref[...]) * (rg * _sigmoid(rg))).astype(y_ref.dtype)


def gla(pj, a, wup, ba, g, batch, seq, tb=512):
    m = pj.shape[0]
    tb = min(tb, seq)
    nt = seq // tb
    nc = tb // CHUNK
    kq = GLA_HEADS
    vq = (2 * GLA_HEADS * GLA_DK) // GLA_DV
    rq = vq + GLA_HEADS
    tok = lambda b, h, t: b * nt + t
    wup_hi = wup.astype(BF16)
    wup_lo = (wup - wup_hi.astype(F32)).astype(BF16)
    return pl.pallas_call(
        functools.partial(_gla_body, n_chunks=nc),
        grid=(batch, GLA_HEADS, nt),
        in_specs=[
            pl.BlockSpec((tb, GLA_DK), lambda b, h, t: (tok(b, h, t), h)),
            pl.BlockSpec((tb, GLA_DK), lambda b, h, t: (tok(b, h, t), kq + h)),
            pl.BlockSpec((tb, GLA_DV), lambda b, h, t: (tok(b, h, t), vq + h)),
            pl.BlockSpec((tb, GLA_DV), lambda b, h, t: (tok(b, h, t), rq + h)),
            pl.BlockSpec((tb, LANES), lambda b, h, t: (tok(b, h, t), 0)),
            pl.BlockSpec((LANES, GLA_DK), lambda b, h, t: (0, h)),
            pl.BlockSpec((LANES, GLA_DK), lambda b, h, t: (0, h)),
            pl.BlockSpec((1, GLA_DK), lambda b, h, t: (0, h)),
            pl.BlockSpec((1, GLA_DV), lambda b, h, t: (0, h)),
        ],
        out_specs=pl.BlockSpec((tb, GLA_DV), lambda b, h, t: (tok(b, h, t), h)),
        out_shape=jax.ShapeDtypeStruct((m, GLA_HEADS * GLA_DV), BF16),
        scratch_shapes=[pltpu.VMEM((GLA_DV, GLA_DK), F32),
                        pltpu.VMEM((tb, GLA_DK), F32),
                        pltpu.VMEM((tb, GLA_DK), BF16),
                        pltpu.VMEM((nc, GLA_DV, GLA_DK), F32),
                        pltpu.VMEM((tb, GLA_DV), F32),
                        pltpu.VMEM((2, CHUNK, GLA_DK), F32),
                        pltpu.VMEM((2, CHUNK, GLA_DK), F32)],
        compiler_params=_cparams("parallel", "parallel", "arbitrary"),
        name="gla",
    )(pj, pj, pj, pj, a, wup_hi, wup_lo, ba, g)


def _conv_body(b_ref, c_ref, u_ref, w_ref, g_ref, y_ref, tail_ref):
    @pl.when(pl.program_id(1) == 0)
    def _():
        tail_ref[...] = jnp.zeros_like(tail_ref)

    tb = b_ref.shape[0]
    row = lax.broadcasted_iota(jnp.int32, (tb, CONV_GROUP_W), 0)
    for gi in range(CONV_GROUPS):
        cols = slice(gi * CONV_GROUP_W, (gi + 1) * CONV_GROUP_W)
        cu = c_ref[:, cols].astype(F32) * u_ref[:, cols].astype(F32)
        t1 = tail_ref[pl.ds(7, 1), cols]
        t2 = tail_ref[pl.ds(6, 1), cols]
        cu1 = jnp.where(row == 0, t1, pltpu.roll(cu, 1, axis=0))
        cu2 = jnp.where(row == 0, t2, jnp.where(row == 1, t1, pltpu.roll(cu, 2, axis=0)))
        conv = (w_ref[pl.ds(0, 1), cols] * cu2 + w_ref[pl.ds(1, 1), cols] * cu1
                + w_ref[pl.ds(2, 1), cols] * cu)
        yb = b_ref[:, cols].astype(F32) * conv
        y_ref[:, cols] = _rms(yb, g_ref[:, cols]).astype(y_ref.dtype)
        tail_ref[:, cols] = cu[tb - 8:, :]


def sconv(pj, w, g, batch, seq, tb=256):
    m = pj.shape[0]
    cw = CONV_GROUPS * CONV_GROUP_W
    tb = min(tb, seq)
    nt = seq // tb
    first = (2 * GLA_HEADS * GLA_DK + 2 * GLA_HEADS * GLA_DV) // cw
    spec = lambda k: pl.BlockSpec((tb, cw), lambda b, t: (b * nt + t, first + k))
    return pl.pallas_call(
        _conv_body,
        grid=(batch, nt),
        in_specs=[spec(0), spec(1), spec(2),
                  pl.BlockSpec((8, cw), lambda b, t: (0, 0)),
                  pl.BlockSpec((1, cw), lambda b, t: (0, 0))],
        out_specs=pl.BlockSpec((tb, cw), lambda b, t: (b * nt + t, 0)),
        out_shape=jax.ShapeDtypeStruct((m, cw), BF16),
        scratch_shapes=[pltpu.VMEM((8, cw), F32)],
        compiler_params=_cparams("parallel", "arbitrary"),
        name="sconv",
    )(pj, pj, pj, w, g)


def _out_body(ya_ref, yb_ref, wa_ref, wb_ref, h_ref, o_ref):
    acc = jnp.dot(ya_ref[...], wa_ref[...], preferred_element_type=F32)
    acc = acc + jnp.dot(yb_ref[...], wb_ref[...], preferred_element_type=F32)
    o_ref[...] = h_ref[...] + acc


def out_proj(ya, yb, w, h, tm=1024, tn=512):
    m, ka = ya.shape
    kb = yb.shape[1]
    n = w.shape[1]
    tm = min(tm, m)
    nka = ka // kb
    return pl.pallas_call(
        _out_body,
        grid=(m // tm, n // tn),
        in_specs=[pl.BlockSpec((tm, ka), lambda i, j: (i, 0)),
                  pl.BlockSpec((tm, kb), lambda i, j: (i, 0)),
                  pl.BlockSpec((ka, tn), lambda i, j: (0, j)),
                  pl.BlockSpec((kb, tn), lambda i, j: (nka, j)),
                  pl.BlockSpec((tm, tn), lambda i, j: (i, j))],
        out_specs=pl.BlockSpec((tm, tn), lambda i, j: (i, j)),
        out_shape=jax.ShapeDtypeStruct((m, n), F32),
        compiler_params=_cparams("parallel", "arbitrary"),
        name="out_proj",
    )(ya, yb, w, w, h)


def _swiglu_step(xn, wg_ref, wu_ref, wd_ref, o_ref):
    gg = jnp.dot(xn, wg_ref[...], preferred_element_type=F32)
    uu = jnp.dot(xn, wu_ref[...], preferred_element_type=F32)
    act = (gg * _sigmoid(gg) * uu).astype(BF16)
    d = o_ref.shape[1]
    for n in range(d // DOWN_SLAB):
        cols = slice(n * DOWN_SLAB, (n + 1) * DOWN_SLAB)
        o_ref[:, cols] += jnp.dot(act, wd_ref[:, cols], preferred_element_type=F32)


def _ffn_body(h_hbm, g_ref, wg_ref, wu_ref, wd_ref, o_ref, xn_ref, sem):
    tm = o_ref.shape[0]

    @pl.when(pl.program_id(1) == 0)
    def _():
        r0 = pl.multiple_of(pl.program_id(0) * tm, tm)
        cp = pltpu.make_async_copy(h_hbm.at[pl.ds(r0, tm), :], o_ref, sem)
        cp.start()
        cp.wait()
        _rms_rows(o_ref, g_ref, xn_ref)

    _swiglu_step(xn_ref[...], wg_ref, wu_ref, wd_ref, o_ref)


def ffn(h, g, wg, wu, wd, tm=512, tf=512):
    m, d = h.shape
    f = wg.shape[1]
    tm = min(tm, m)
    return pl.pallas_call(
        _ffn_body,
        grid=(m // tm, f // tf),
        in_specs=[pl.BlockSpec(memory_space=pl.ANY),
                  pl.BlockSpec((1, d), lambda i, j: (0, 0)),
                  pl.BlockSpec((d, tf), lambda i, j: (0, j)),
                  pl.BlockSpec((d, tf), lambda i, j: (0, j)),
                  pl.BlockSpec((tf, d), lambda i, j: (j, 0))],
        out_specs=pl.BlockSpec((tm, d), lambda i, j: (i, 0)),
        out_shape=jax.ShapeDtypeStruct((m, d), F32),
        scratch_shapes=[pltpu.VMEM((tm, d), BF16), pltpu.SemaphoreType.DMA(())],
        compiler_params=_cparams("parallel", "arbitrary"),
        name="ffn",
    )(h, g, wg, wu, wd)


def _router_body(h_ref, g_ref, rwh_ref, rwl_ref, rb_ref, info_ref, gate_ref, cnt_ref, carry_ref):
    @pl.when(pl.program_id(0) == 0)
    def _():
        carry_ref[...] = jnp.zeros_like(carry_ref)

    tm = h_ref.shape[0]
    hn = _rms(h_ref[...], g_ref[...])
    hi = hn.astype(BF16)
    lo = (hn - hi.astype(F32)).astype(BF16)
    logits = (jnp.dot(hi, rwh_ref[...], preferred_element_type=F32)
              + jnp.dot(lo, rwh_ref[...], preferred_element_type=F32)
              + jnp.dot(hi, rwl_ref[...], preferred_element_type=F32)) + rb_ref[...]
    col = lax.broadcasted_iota(jnp.int32, (tm, LANES), 1)
    m1 = jnp.max(logits, axis=-1, keepdims=True)
    i1 = jnp.min(jnp.where(logits == m1, col, LANES), axis=-1, keepdims=True)
    rest = jnp.where(col == i1, 2.0 * NEG_BIG, logits)
    m2 = jnp.max(rest, axis=-1, keepdims=True)
    i2 = jnp.min(jnp.where(rest == m2, col, LANES), axis=-1, keepdims=True)
    e21 = jnp.exp(m2 - m1)
    g1 = 1.0 / (1.0 + e21)
    g2 = e21 / (1.0 + e21)

    sel1 = col == i1
    sel2 = col == i2
    onehot = jnp.where(sel1, 1.0, jnp.where(sel2, 1.0, 0.0))
    r = lax.broadcasted_iota(jnp.int32, (tm, tm), 0)
    c = lax.broadcasted_iota(jnp.int32, (tm, tm), 1)
    strict = jnp.where(r > c, 1.0, 0.0).astype(BF16)
    before = jnp.dot(strict, onehot.astype(BF16), preferred_element_type=F32) + carry_ref[...]
    rank1 = jnp.sum(jnp.where(sel1, before, 0.0), axis=-1, keepdims=True).astype(jnp.int32)
    rank2 = jnp.sum(jnp.where(sel2, before, 0.0), axis=-1, keepdims=True).astype(jnp.int32)
    carry_ref[...] = carry_ref[...] + jnp.sum(onehot, axis=0, keepdims=True)

    info_ref[...] = jnp.where(col == 0, i1, jnp.where(col == 1, i2,
                              jnp.where(col == 2, rank1, jnp.where(col == 3, rank2, 0))))
    gate_ref[...] = jnp.where(col == 0, g1, jnp.where(col == 1, g2, 0.0))
    cnt_ref[...] = carry_ref[...]


def router(h, g, rw, rb, tm=256):
    m, d = h.shape
    tm = min(tm, m)
    rw_hi = rw.astype(BF16)
    rw_lo = (rw - rw_hi.astype(F32)).astype(BF16)
    return pl.pallas_call(
        _router_body,
        grid=(m // tm,),
        in_specs=[pl.BlockSpec((tm, d), lambda i: (i, 0)),
                  pl.BlockSpec((1, d), lambda i: (0, 0)),
                  pl.BlockSpec((d, LANES), lambda i: (0, 0)),
                  pl.BlockSpec((d, LANES), lambda i: (0, 0)),
                  pl.BlockSpec((1, LANES), lambda i: (0, 0))],
        out_specs=[pl.BlockSpec((tm, LANES), lambda i: (i, 0)),
                   pl.BlockSpec((tm, LANES), lambda i: (i, 0)),
                   pl.BlockSpec((1, LANES), lambda i: (0, 0))],
        out_shape=[jax.ShapeDtypeStruct((m, LANES), jnp.int32),
                   jax.ShapeDtypeStruct((m, LANES), F32),
                   jax.ShapeDtypeStruct((1, LANES), F32)],
        scratch_shapes=[pltpu.VMEM((1, LANES), F32)],
        compiler_params=_cparams("arbitrary"),
        name="router",
    )(h, g, rw_hi, rw_lo, rb)


def _row_copy(src_hbm, dst_vmem, sem, tok, r):
    return pltpu.make_async_copy(src_hbm.at[pl.ds(tok, 1), :],
                                 dst_vmem.at[pl.ds(r, 1), :], sem)


def _issue_rows(src_hbm, dst_vmem, sem, idx_ref, first, count):
    def body(k, carry):
        r = first + k
        _row_copy(src_hbm, dst_vmem, sem, idx_ref[0, 0, r], r).start()
        return carry

    lax.fori_loop(0, count, body, 0, unroll=8)


def _wait_rows(src_hbm, dst_vmem, sem):
    n = dst_vmem.shape[0]
    pltpu.make_async_copy(src_hbm.at[pl.ds(0, n), :], dst_vmem, sem).wait()


def _expert_body(te_ref, nv_ref, idx_ref, h_hbm, g_ref, gs_ref, wg_ref, wu_ref, wd_ref,
                 o_ref, xbuf_ref, xn_ref, sem, *, issue_steps):
    i = pl.program_id(0)
    j = pl.program_id(1)
    tm = o_ref.shape[0]
    per_step = tm // issue_steps

    @pl.when((i == 0) & (j == 0))
    def _():
        _issue_rows(h_hbm, xbuf_ref, sem, idx_ref, 0, tm)

    @pl.when(j == 0)
    def _():
        _wait_rows(h_hbm, xbuf_ref, sem)
        _rms_rows(xbuf_ref, g_ref, xn_ref)
        o_ref[...] = jnp.zeros_like(o_ref)

    @pl.when((j >= 1) & (j <= issue_steps) & (i + 1 < pl.num_programs(0)))
    def _():
        _issue_rows(h_hbm, xbuf_ref, sem, idx_ref, (j - 1) * per_step, per_step)

    @pl.when(i < nv_ref[0])
    def _():
        _swiglu_step(xn_ref[...], wg_ref, wu_ref, wd_ref, o_ref)

    @pl.when(j == pl.num_programs(1) - 1)
    def _():
        def scale(rows):
            o_ref[rows, :] = o_ref[rows, :] * gs_ref[rows, :]

        _for_row_slabs(o_ref, scale)


def expert_ffn(h, g, src, gs, tile_expert, n_valid, wg, wu, wd, tm, tf=256):
    p = src.shape[0]
    d = h.shape[1]
    f = wg.shape[2]
    nf = f // tf
    nt = p // tm
    issue_steps = min(8, nf - 1)

    def fblk(i, j, nv):
        return jnp.where(i < nv[0], j, nf - 1)

    def idx_blk(i, j, te, nv):
        return (jnp.where((i == 0) & (j == 0), 0, jnp.minimum(i + 1, nt - 1)), 0, 0)

    grid_spec = pltpu.PrefetchScalarGridSpec(
        num_scalar_prefetch=2,
        grid=(nt, nf),
        in_specs=[pl.BlockSpec((1, 1, tm), idx_blk, memory_space=pltpu.SMEM),
                  pl.BlockSpec(memory_space=pl.ANY),
                  pl.BlockSpec((1, d), lambda i, j, te, nv: (0, 0)),
                  pl.BlockSpec((tm, 1), lambda i, j, te, nv: (i, 0)),
                  pl.BlockSpec((None, d, tf), lambda i, j, te, nv: (te[i], 0, fblk(i, j, nv))),
                  pl.BlockSpec((None, d, tf), lambda i, j, te, nv: (te[i], 0, fblk(i, j, nv))),
                  pl.BlockSpec((None, tf, d), lambda i, j, te, nv: (te[i], fblk(i, j, nv), 0))],
        out_specs=pl.BlockSpec((tm, d), lambda i, j, te, nv: (i, 0)),
        scratch_shapes=[pltpu.VMEM((tm, d), F32), pltpu.VMEM((tm, d), BF16),
                        pltpu.SemaphoreType.DMA(())],
    )
    return pl.pallas_call(
        functools.partial(_expert_body, issue_steps=issue_steps),
        grid_spec=grid_spec,
        out_shape=jax.ShapeDtypeStruct((p, d), F32),
        compiler_params=_cparams("arbitrary", "arbitrary"),
        name="moe_experts",
    )(tile_expert, n_valid, src.reshape(nt, 1, tm), h, g, gs, wg, wu, wd)


def moe_dispatch(h, g, rw, rb, wg, wu, wd, tm=512):
    m, d = h.shape
    tm = min(tm, m)
    info, gates, counts = router(h, g, rw, rb)
    e1, e2, r1, r2 = info[:, 0], info[:, 1], info[:, 2], info[:, 3]
    g1, g2 = gates[:, 0], gates[:, 1]
    counts = counts[0, :N_EXPERTS].astype(jnp.int32)

    tiles_e = (counts + tm - 1) // tm
    tile_end = jnp.cumsum(tiles_e)
    row_off = (tile_end - tiles_e) * tm
    n_tiles = (2 * m) // tm + N_EXPERTS
    p = n_tiles * tm
    pos1 = row_off[e1] + r1
    pos2 = row_off[e2] + r2
    pos = jnp.concatenate([pos1, pos2])
    tok = jnp.arange(m, dtype=jnp.int32)
    src = jnp.zeros((p,), jnp.int32).at[pos].set(jnp.concatenate([tok, tok]))
    gs = jnp.zeros((p,), F32).at[pos].set(jnp.concatenate([g1, g2]))
    n_valid = tile_end[-1:].astype(jnp.int32)
    tile_id = jnp.minimum(jnp.arange(n_tiles, dtype=jnp.int32), n_valid[0] - 1)
    tile_expert = jnp.sum(tile_id[:, None] >= tile_end[None, :], axis=1).astype(jnp.int32)

    ys = expert_ffn(h, g, src, gs.reshape(p, 1), tile_expert, n_valid, wg, wu, wd, tm)
    return ys, pos1, pos2


def _ple_body(*refs, combine, last):
    refs = list(refs)
    if combine:
        p1c_ref, p2c_ref, p1n_ref, p2n_ref, ys_hbm = refs[:5]
        refs = refs[5:]
    h_ref, g_ref, gd_ref, gu_ref, p_ref, pw_ref, gn_ref = refs[:7]
    refs = refs[7:]
    if not last:
        wa_ref = refs.pop(0)
    o_ref = refs.pop(0)
    if not last:
        xn_ref, a_ref = refs.pop(0), refs.pop(0)

    h = h_ref[...]
    if combine:
        ab_ref, sem = refs
        i = pl.program_id(0)
        tc = h_ref.shape[0]

        def issue(p1_ref, p2_ref):
            _issue_rows(ys_hbm, ab_ref.at[0], sem.at[0], p1_ref, 0, tc)
            _issue_rows(ys_hbm, ab_ref.at[1], sem.at[1], p2_ref, 0, tc)

        @pl.when(i == 0)
        def _():
            issue(p1c_ref, p2c_ref)

        _wait_rows(ys_hbm, ab_ref.at[0], sem.at[0])
        _wait_rows(ys_hbm, ab_ref.at[1], sem.at[1])
        h = h + ab_ref[0] + ab_ref[1]
        o_ref[...] = h

        @pl.when(i + 1 < pl.num_programs(0))
        def _():
            issue(p1n_ref, p2n_ref)

        h = o_ref[...]

    gn = _rms(h, g_ref[...]).astype(BF16)
    t = jnp.dot(gn, gd_ref[...], preferred_element_type=F32).astype(BF16)
    gate = _sigmoid(jnp.dot(t, gu_ref[...], preferred_element_type=F32))
    emb = jnp.dot(p_ref[...].astype(BF16), pw_ref[...], preferred_element_type=F32)
    hn = h + gate * emb
    if last:
        o_ref[...] = _rms(hn, gn_ref[...])
    else:
        o_ref[...] = hn
        xn = _rms(hn, gn_ref[...]).astype(BF16)
        xn_ref[...] = xn
        a_ref[...] = jnp.dot(xn, wa_ref[...], preferred_element_type=F32)


def ple(h, g, gd, gu, p, pw, g_next, wa_next=None, moe_parts=None, tm=256):
    m, d = h.shape
    e = p.shape[1]
    tm = min(tm, m)
    nt = m // tm
    last = wa_next is None
    combine = moe_parts is not None
    row = lambda w: pl.BlockSpec((tm, w), lambda i: (i, 0))
    full = lambda a, b: pl.BlockSpec((a, b), lambda i: (0, 0))
    in_specs, args, scratch = [], [], []
    if combine:
        ys, pos1, pos2 = moe_parts
        cur = pl.BlockSpec((1, 1, tm), lambda i: (i, 0, 0), memory_space=pltpu.SMEM)
        nxt = pl.BlockSpec((1, 1, tm), lambda i: (jnp.minimum(i + 1, nt - 1), 0, 0),
                           memory_space=pltpu.SMEM)
        p1, p2 = pos1.reshape(nt, 1, tm), pos2.reshape(nt, 1, tm)
        in_specs += [cur, cur, nxt, nxt, pl.BlockSpec(memory_space=pl.ANY)]
        args += [p1, p2, p1, p2, ys]
        scratch = [pltpu.VMEM((2, tm, d), F32), pltpu.SemaphoreType.DMA((2,))]
    in_specs += [row(d), full(1, d), full(d, e), full(e, d), row(e), full(e, d), full(1, d)]
    args += [h, g, gd, gu, p, pw, g_next]
    if last:
        out_specs = row(d)
        out_shape = jax.ShapeDtypeStruct((m, d), F32)
    else:
        in_specs.append(full(d, LANES))
        args.append(wa_next)
        out_specs = [row(d), row(d), row(LANES)]
        out_shape = [jax.ShapeDtypeStruct((m, d), F32), jax.ShapeDtypeStruct((m, d), BF16),
                     jax.ShapeDtypeStruct((m, LANES), F32)]
    return pl.pallas_call(
        functools.partial(_ple_body, combine=combine, last=last),
        grid=(nt,), in_specs=in_specs, out_specs=out_specs, out_shape=out_shape,
        scratch_shapes=scratch,
        compiler_params=_cparams("arbitrary" if combine else "parallel"),
        name="ple_final" if last else "ple_next",
    )(*args)


def kernel(x, p, ln_mix_g, w_in, w_a_up, b_a, gla_g, conv_w, conv_g, w_out, ln_ffn_g, ffn_wg, ffn_wu, ffn_wd, router_w, router_b, exp_wg, exp_wu, exp_wd, ln_ple_g, ple_gd, ple_gu, ple_proj, final_g):
    batch, seq, d = x.shape
    depth = w_in.shape[0]
    m = batch * seq
    a0 = 2 * GLA_HEADS * GLA_DK + 2 * GLA_HEADS * GLA_DV
    row2 = lambda v: v.reshape(1, -1)

    def gate_in_w(i):
        return jnp.pad(w_in[i][:, a0:a0 + GATE_RANK], ((0, 0), (0, LANES - GATE_RANK))).astype(BF16)

    h = x.reshape(m, d)
    xn, a = norm_a(h, row2(ln_mix_g[0]), gate_in_w(0))
    out = None
    for i in range(depth):
        w_main = jnp.concatenate([w_in[i][:, :a0], w_in[i][:, a0 + GATE_RANK:]], axis=1).astype(BF16)
        pj = proj(xn, w_main)
        wup = jnp.pad(w_a_up[i], ((0, LANES - GATE_RANK), (0, 0)))
        ya = gla(pj, a, wup, row2(b_a[i]), row2(gla_g[i]), batch, seq)
        cw = jnp.pad(conv_w[i], ((0, 8 - conv_w.shape[1]), (0, 0)))
        yb = sconv(pj, cw, row2(conv_g[i]), batch, seq)
        h = out_proj(ya, yb, w_out[i].astype(BF16), h)

        j = i // 2
        if i % 2 == 0:
            fpad = (-ffn_wg.shape[2]) % FFN_TF
            h = ffn(h, row2(ln_ffn_g[i]),
                    jnp.pad(ffn_wg[j].astype(BF16), ((0, 0), (0, fpad))),
                    jnp.pad(ffn_wu[j].astype(BF16), ((0, 0), (0, fpad))),
                    jnp.pad(ffn_wd[j].astype(BF16), ((0, fpad), (0, 0))), tf=FFN_TF)
            moe_parts = None
        else:
            rw = jnp.pad(router_w[j], ((0, 0), (0, LANES - N_EXPERTS)))
            rb = jnp.pad(router_b[j], (0, LANES - N_EXPERTS), constant_values=NEG_BIG)
            moe_parts = moe_dispatch(h, row2(ln_ffn_g[i]), rw, row2(rb), exp_wg[j].astype(BF16),
                                     exp_wu[j].astype(BF16), exp_wd[j].astype(BF16))

        ple_args = (h, row2(ln_ple_g[i]), ple_gd[i].astype(BF16), ple_gu[i].astype(BF16),
                    p[i].reshape(m, -1), ple_proj[i].astype(BF16))
        if i + 1 < depth:
            h, xn, a = ple(*ple_args, row2(ln_mix_g[i + 1]), gate_in_w(i + 1), moe_parts=moe_parts)
        else:
            out = ple(*ple_args, row2(final_g), moe_parts=moe_parts)
    return out.reshape(batch, seq, d)
```

```python
import functools

import jax
import jax.numpy as jnp
from jax import lax
from jax.experimental import pallas as pl
from jax.experimental.pallas import tpu as pltpu

F32 = jnp.float32
BF16 = jnp.bfloat16
HIGHEST = lax.Precision.HIGHEST

EPS = 1e-6
LANES = 128
GLA_HEADS = 8
GLA_DK = 128
GLA_DV = 256
GATE_RANK = 16
GATE_COL = 2 * GLA_HEADS * GLA_DK + 2 * GLA_HEADS * GLA_DV
GATE_TAU = 16.0
CHUNK = 64
SUB = 16
N_SUB = CHUNK // SUB
CONV_GROUPS = 16
CONV_GROUP_W = 128
N_EXPERTS = 8
LOG2E = 1.4426950408889634
MAX_SUB_DECAY_LOG2 = 100.0
ROW_SLAB = 128
DOWN_SLAB = 512
FFN_TF = 256
NEG_BIG = -1e30
VMEM_LIMIT = 56 * 1024 * 1024


def _cparams(*sem):
    return pltpu.CompilerParams(dimension_semantics=sem, vmem_limit_bytes=VMEM_LIMIT)


def _rms(x, g):
    ms = jnp.mean(x * x, axis=-1, keepdims=True)
    return x * lax.rsqrt(ms + EPS) * g


def _sigmoid(x):
    return 1.0 / (1.0 + jnp.exp(-x))


def _for_row_slabs(ref, fn):
    n = ref.shape[0]
    slab = min(ROW_SLAB, n)

    def body(c, carry):
        fn(pl.ds(pl.multiple_of(c * slab, slab), slab))
        return carry

    lax.fori_loop(0, n // slab, body, 0)


def _rms_rows(src_ref, g_ref, dst_ref):
    def slab(rows):
        dst_ref[rows, :] = _rms(src_ref[rows, :], g_ref[...]).astype(dst_ref.dtype)

    _for_row_slabs(src_ref, slab)


def _gate_proj(xn, wa_ref):
    a = jnp.dot(xn, wa_ref[...].astype(BF16), preferred_element_type=F32)
    col = lax.broadcasted_iota(jnp.int32, a.shape, 1)
    return jnp.where(col < GATE_RANK, a, 0.0)


def _gate_block_spec(d):
    blk = GATE_COL // LANES
    return pl.BlockSpec((d, LANES), lambda *_: (0, blk))


def _norm_a_body(h_ref, g_ref, wa_ref, xn_ref, a_ref):
    xn = _rms(h_ref[...], g_ref[...]).astype(BF16)
    xn_ref[...] = xn
    a_ref[...] = _gate_proj(xn, wa_ref)


def norm_a(h, g, wa, tm=256):
    m, d = h.shape
    tm = min(tm, m)
    return pl.pallas_call(
        _norm_a_body,
        grid=(m // tm,),
        in_specs=[pl.BlockSpec((tm, d), lambda i: (i, 0)),
                  pl.BlockSpec((1, d), lambda i: (0, 0)),
                  _gate_block_spec(d)],
        out_specs=[pl.BlockSpec((tm, d), lambda i: (i, 0)),
                   pl.BlockSpec((tm, LANES), lambda i: (i, 0))],
        out_shape=[jax.ShapeDtypeStruct((m, d), BF16),
                   jax.ShapeDtypeStruct((m, LANES), F32)],
        compiler_params=_cparams("parallel"),
        name="norm_a",
    )(h, g, wa)


def _proj_body(x_ref, w_ref, o_ref):
    o_ref[...] = jnp.dot(x_ref[...], w_ref[...],
                         preferred_element_type=F32).astype(o_ref.dtype)


def proj(x, w, tm=1024, tn=1024):
    m, d = x.shape
    n = w.shape[1]
    tm = min(tm, m)
    return pl.pallas_call(
        _proj_body,
        grid=(m // tm, n // tn),
        in_specs=[pl.BlockSpec((tm, d), lambda i, j: (i, 0)),
                  pl.BlockSpec((d, tn), lambda i, j: (0, j))],
        out_specs=pl.BlockSpec((tm, tn), lambda i, j: (i, j)),
        out_shape=jax.ShapeDtypeStruct((m, n), BF16),
        compiler_params=_cparams("parallel", "arbitrary"),
        name="in_proj",
    )(x, w)


def _gla_body(q_ref, k_ref, v_ref, r_ref, a_ref, wuh_ref, wul_ref, ba_ref, g_ref, y_ref,
              st_ref, bb_ref, qb_ref, u_ref, o_ref, ks_ref, bs_ref, *, n_chunks):
    @pl.when(pl.program_id(2) == 0)
    def _():
        st_ref[...] = jnp.zeros_like(st_ref)

    row = lax.broadcasted_iota(jnp.int32, (CHUNK, CHUNK), 0)
    col = lax.broadcasted_iota(jnp.int32, (CHUNK, CHUNK), 1)
    dloc = col - (row // SUB) * SUB
    dsel = jnp.where((dloc >= 0) & (dloc <= row % SUB), dloc, -1)
    krow = lax.broadcasted_iota(jnp.int32, (CHUNK, GLA_DK), 0)
    scale = GLA_DK ** -0.5

    a_hi = a_ref[...].astype(BF16)
    a_lo = (a_ref[...] - a_hi.astype(F32)).astype(BF16)
    z = (jnp.dot(a_hi, wuh_ref[...], preferred_element_type=F32)
         + jnp.dot(a_lo, wuh_ref[...], preferred_element_type=F32)
         + jnp.dot(a_hi, wul_ref[...], preferred_element_type=F32)) + ba_ref[...]
    csum = (jnp.minimum(z, 0.0) - jnp.log(1.0 + jnp.exp(-jnp.abs(z)))) * (LOG2E / GATE_TAU)
    crow = lax.broadcasted_iota(jnp.int32, csum.shape, 0) % CHUNK
    shift = 1
    while shift < CHUNK:
        csum = csum + jnp.where(crow >= shift, pltpu.roll(csum, shift, axis=0), 0.0)
        shift *= 2
    bb_ref[...] = csum

    sub_decay = jnp.where(crow % SUB == SUB - 1,
                          pltpu.roll(csum, SUB - 1, axis=0) - csum, 0.0)
    matmul_safe = jnp.max(sub_decay) < MAX_SUB_DECAY_LOG2

    def intra(c, slot, use_matmul):
        r0 = pl.multiple_of(c * CHUNK, CHUNK)
        rows = pl.ds(r0, CHUNK)
        qc = q_ref[rows, :].astype(F32) * scale
        kc = k_ref[rows, :].astype(F32)
        vc = v_ref[rows, :]
        bc = bb_ref[rows, :]

        if use_matmul:
            blocks = []
            for i in range(N_SUB):
                sub = slice(i * SUB, (i + 1) * SUB)
                ref_i = bc[i * SUB:i * SUB + 1]
                qs = qc[sub] * jnp.exp2(bc[sub] - ref_i)
                kk = jnp.where(krow < (i + 1) * SUB, kc * jnp.exp2(ref_i - bc), 0.0)
                blocks.append(lax.dot_general(
                    qs.astype(BF16), kk.astype(BF16), (((1,), (1,)), ((), ())),
                    preferred_element_type=F32))
            amat = jnp.where(row >= col, jnp.concatenate(blocks, axis=0), 0.0)
        else:
            ks = ks_ref.at[slot]
            bs = bs_ref.at[slot]
            ks[...] = kc
            bs[...] = bc
            amat = jnp.zeros((CHUNK, CHUNK), F32)
            for j in range(SUB):
                kj = jnp.concatenate(
                    [jnp.broadcast_to(ks[pl.ds(i * SUB + j, 1), :], (SUB, GLA_DK))
                     for i in range(N_SUB)], axis=0)
                bj = jnp.concatenate(
                    [jnp.broadcast_to(bs[pl.ds(i * SUB + j, 1), :], (SUB, GLA_DK))
                     for i in range(N_SUB)], axis=0)
                pj = qc * jnp.exp2(bc - bj) * kj
                amat = jnp.where(dsel == j, jnp.sum(pj, axis=-1, keepdims=True), amat)
            off_rows = [jnp.zeros((SUB, CHUNK), F32)]
            for i in range(1, N_SUB):
                sub = slice(i * SUB, (i + 1) * SUB)
                ref_i = bc[i * SUB:i * SUB + 1]
                qs = qc[sub] * jnp.exp2(bc[sub] - ref_i)
                kk = jnp.where(krow < i * SUB, kc * jnp.exp2(ref_i - bc), 0.0)
                off_rows.append(lax.dot_general(
                    qs.astype(BF16), kk.astype(BF16), (((1,), (1,)), ((), ())),
                    preferred_element_type=F32))
            amat = amat + jnp.concatenate(off_rows, axis=0)

        qb_ref[rows, :] = (qc * jnp.exp2(bc)).astype(BF16)
        b_last = bc[CHUNK - 1:CHUNK]
        kd = (kc * jnp.exp2(b_last - bc)).astype(BF16)
        u_ref[c] = lax.dot_general(vc, kd, (((0,), (0,)), ((), ())),
                                   preferred_element_type=F32)
        return amat.astype(BF16)

    def intra_out(c, amat):
        rows = pl.ds(pl.multiple_of(c * CHUNK, CHUNK), CHUNK)
        o_ref[rows, :] = jnp.dot(amat, v_ref[rows, :], preferred_element_type=F32)

    amats = [intra(c, 0, True) for c in range(n_chunks)]
    for c in range(n_chunks):
        intra_out(c, amats[c])

    @pl.when(jnp.logical_not(matmul_safe))
    def _():
        def pair(i, carry):
            a0 = intra(2 * i, 0, False)
            a1 = intra(2 * i + 1, 1, False)
            intra_out(2 * i, a0)
            intra_out(2 * i + 1, a1)
            return carry

        lax.fori_loop(0, n_chunks // 2, pair, 0)

    st = st_ref[...]
    for c in range(n_chunks):
        rows = slice(c * CHUNK, (c + 1) * CHUNK)
        o_ref[rows, :] += lax.dot_general(qb_ref[rows, :], st.astype(BF16),
                                          (((1,), (1,)), ((), ())),
                                          preferred_element_type=F32)
        st = st * jnp.exp2(bb_ref[(c + 1) * CHUNK - 1:(c + 1) * CHUNK, :]) + u_ref[c]
    st_ref[...] = st

    rg = r_ref[...].astype(F32)
    y_ref[...] = (_rms(o_ref[...], g_ref[...]) * (rg * _sigmoid(rg))).astype(y_ref.dtype)


def gla(pj, a, wup, ba, g, batch, seq, tb=512):
    m = pj.shape[0]
    tb = min(tb, seq)
    nt = seq // tb
    nc = tb // CHUNK
    kq = GLA_HEADS
    vq = (2 * GLA_HEADS * GLA_DK) // GLA_DV
    rq = vq + GLA_HEADS
    tok = lambda b, h, t: b * nt + t
    wup_hi = wup.astype(BF16)
    wup_lo = (wup - wup_hi.astype(F32)).astype(BF16)
    return pl.pallas_call(
        functools.partial(_gla_body, n_chunks=nc),
        grid=(batch, GLA_HEADS, nt),
        in_specs=[
            pl.BlockSpec((tb, GLA_DK), lambda b, h, t: (tok(b, h, t), h)),
            pl.BlockSpec((tb, GLA_DK), lambda b, h, t: (tok(b, h, t), kq + h)),
            pl.BlockSpec((tb, GLA_DV), lambda b, h, t: (tok(b, h, t), vq + h)),
            pl.BlockSpec((tb, GLA_DV), lambda b, h, t: (tok(b, h, t), rq + h)),
            pl.BlockSpec((tb, LANES), lambda b, h, t: (tok(b, h, t), 0)),
            pl.BlockSpec((LANES, GLA_DK), lambda b, h, t: (0, h)),
            pl.BlockSpec((LANES, GLA_DK), lambda b, h, t: (0, h)),
            pl.BlockSpec((1, GLA_DK), lambda b, h, t: (0, h)),
            pl.BlockSpec((1, GLA_DV), lambda b, h, t: (0, h)),
        ],
        out_specs=pl.BlockSpec((tb, GLA_DV), lambda b, h, t: (tok(b, h, t), h)),
        out_shape=jax.ShapeDtypeStruct((m, GLA_HEADS * GLA_DV), BF16),
        scratch_shapes=[pltpu.VMEM((GLA_DV, GLA_DK), F32),
                        pltpu.VMEM((tb, GLA_DK), F32),
                        pltpu.VMEM((tb, GLA_DK), BF16),
                        pltpu.VMEM((nc, GLA_DV, GLA_DK), F32),
                        pltpu.VMEM((tb, GLA_DV), F32),
                        pltpu.VMEM((2, CHUNK, GLA_DK), F32),
                        pltpu.VMEM((2, CHUNK, GLA_DK), F32)],
        compiler_params=_cparams("parallel", "parallel", "arbitrary"),
        name="gla",
    )(pj, pj, pj, pj, a, wup_hi, wup_lo, ba, g)


def _conv_body(b_ref, c_ref, u_ref, w_ref, g_ref, y_ref, tail_ref):
    @pl.when(pl.program_id(1) == 0)
    def _():
        tail_ref[...] = jnp.zeros_like(tail_ref)

    tb = b_ref.shape[0]
    row = lax.broadcasted_iota(jnp.int32, (tb, CONV_GROUP_W), 0)
    for gi in range(CONV_GROUPS):
        cols = slice(gi * CONV_GROUP_W, (gi + 1) * CONV_GROUP_W)
        cu = c_ref[:, cols].astype(F32) * u_ref[:, cols].astype(F32)
        t1 = tail_ref[pl.ds(7, 1), cols]
        t2 = tail_ref[pl.ds(6, 1), cols]
        cu1 = jnp.where(row == 0, t1, pltpu.roll(cu, 1, axis=0))
        cu2 = jnp.where(row == 0, t2, jnp.where(row == 1, t1, pltpu.roll(cu, 2, axis=0)))
        conv = (w_ref[pl.ds(0, 1), cols] * cu2 + w_ref[pl.ds(1, 1), cols] * cu1
                + w_ref[pl.ds(2, 1), cols] * cu)
        yb = b_ref[:, cols].astype(F32) * conv
        y_ref[:, cols] = _rms(yb, g_ref[:, cols]).astype(y_ref.dtype)
        tail_ref[:, cols] = cu[tb - 8:, :]


def sconv(pj, w, g, batch, seq, tb=256):
    m = pj.shape[0]
    cw = CONV_GROUPS * CONV_GROUP_W
    tb = min(tb, seq)
    nt = seq // tb
    first = (2 * GLA_HEADS * GLA_DK + 2 * GLA_HEADS * GLA_DV) // cw
    spec = lambda k: pl.BlockSpec((tb, cw), lambda b, t: (b * nt + t, first + k))
    return pl.pallas_call(
        _conv_body,
        grid=(batch, nt),
        in_specs=[spec(0), spec(1), spec(2),
                  pl.BlockSpec((8, cw), lambda b, t: (0, 0)),
                  pl.BlockSpec((1, cw), lambda b, t: (0, 0))],
        out_specs=pl.BlockSpec((tb, cw), lambda b, t: (b * nt + t, 0)),
        out_shape=jax.ShapeDtypeStruct((m, cw), BF16),
        scratch_shapes=[pltpu.VMEM((8, cw), F32)],
        compiler_params=_cparams("parallel", "arbitrary"),
        name="sconv",
    )(pj, pj, pj, w, g)


def _out_body(ya_ref, yb_ref, wa_ref, wb_ref, h_ref, o_ref):
    acc = jnp.dot(ya_ref[...], wa_ref[...], preferred_element_type=F32)
    acc = acc + jnp.dot(yb_ref[...], wb_ref[...], preferred_element_type=F32)
    o_ref[...] = h_ref[...] + acc


def out_proj(ya, yb, w, h, tm=1024, tn=512):
    m, ka = ya.shape
    kb = yb.shape[1]
    n = w.shape[1]
    tm = min(tm, m)
    nka = ka // kb
    return pl.pallas_call(
        _out_body,
        grid=(m // tm, n // tn),
        in_specs=[pl.BlockSpec((tm, ka), lambda i, j: (i, 0)),
                  pl.BlockSpec((tm, kb), lambda i, j: (i, 0)),
                  pl.BlockSpec((ka, tn), lambda i, j: (0, j)),
                  pl.BlockSpec((kb, tn), lambda i, j: (nka, j)),
                  pl.BlockSpec((tm, tn), lambda i, j: (i, j))],
        out_specs=pl.BlockSpec((tm, tn), lambda i, j: (i, j)),
        out_shape=jax.ShapeDtypeStruct((m, n), F32),
        compiler_params=_cparams("parallel", "arbitrary"),
        name="out_proj",
    )(ya, yb, w, w, h)


def _swiglu_step(xn, wg_ref, wu_ref, wd_ref, o_ref):
    gg = jnp.dot(xn, wg_ref[...], preferred_element_type=F32)
    uu = jnp.dot(xn, wu_ref[...], preferred_element_type=F32)
    act = (gg * _sigmoid(gg) * uu).astype(BF16)
    d = o_ref.shape[1]
    for n in range(d // DOWN_SLAB):
        cols = slice(n * DOWN_SLAB, (n + 1) * DOWN_SLAB)
        o_ref[:, cols] += jnp.dot(act, wd_ref[:, cols], preferred_element_type=F32)


def _ffn_body(h_hbm, g_ref, wg_ref, wu_ref, wd_ref, o_ref, xn_ref, sem):
    tm = o_ref.shape[0]

    @pl.when(pl.program_id(1) == 0)
    def _():
        r0 = pl.multiple_of(pl.program_id(0) * tm, tm)
        cp = pltpu.make_async_copy(h_hbm.at[pl.ds(r0, tm), :], o_ref, sem)
        cp.start()
        cp.wait()
        _rms_rows(o_ref, g_ref, xn_ref)

    _swiglu_step(xn_ref[...], wg_ref, wu_ref, wd_ref, o_ref)


def ffn(h, g, wg, wu, wd, tm=512, tf=FFN_TF):
    m, d = h.shape
    f = wg.shape[1]
    tm = min(tm, m)
    return pl.pallas_call(
        _ffn_body,
        grid=(m // tm, f // tf),
        in_specs=[pl.BlockSpec(memory_space=pl.ANY),
                  pl.BlockSpec((1, d), lambda i, j: (0, 0)),
                  pl.BlockSpec((d, tf), lambda i, j: (0, j)),
                  pl.BlockSpec((d, tf), lambda i, j: (0, j)),
                  pl.BlockSpec((tf, d), lambda i, j: (j, 0))],
        out_specs=pl.BlockSpec((tm, d), lambda i, j: (i, 0)),
        out_shape=jax.ShapeDtypeStruct((m, d), F32),
        scratch_shapes=[pltpu.VMEM((tm, d), BF16), pltpu.SemaphoreType.DMA(())],
        compiler_params=_cparams("parallel", "arbitrary"),
        name="ffn",
    )(h, g, wg, wu, wd)


def _router_body(h_ref, g_ref, rwh_ref, rwl_ref, rb_ref, info_ref, gate_ref, cnt_ref, carry_ref):
    @pl.when(pl.program_id(0) == 0)
    def _():
        carry_ref[...] = jnp.zeros_like(carry_ref)

    tm = h_ref.shape[0]
    hn = _rms(h_ref[...], g_ref[...])
    hi = hn.astype(BF16)
    lo = (hn - hi.astype(F32)).astype(BF16)
    logits = (jnp.dot(hi, rwh_ref[...], preferred_element_type=F32)
              + jnp.dot(lo, rwh_ref[...], preferred_element_type=F32)
              + jnp.dot(hi, rwl_ref[...], preferred_element_type=F32)) + rb_ref[...]
    col = lax.broadcasted_iota(jnp.int32, (tm, LANES), 1)
    m1 = jnp.max(logits, axis=-1, keepdims=True)
    i1 = jnp.min(jnp.where(logits == m1, col, LANES), axis=-1, keepdims=True)
    rest = jnp.where(col == i1, 2.0 * NEG_BIG, logits)
    m2 = jnp.max(rest, axis=-1, keepdims=True)
    i2 = jnp.min(jnp.where(rest == m2, col, LANES), axis=-1, keepdims=True)
    e21 = jnp.exp(m2 - m1)
    g1 = 1.0 / (1.0 + e21)
    g2 = e21 / (1.0 + e21)

    sel1 = col == i1
    sel2 = col == i2
    onehot = jnp.where(sel1, 1.0, jnp.where(sel2, 1.0, 0.0))
    r = lax.broadcasted_iota(jnp.int32, (tm, tm), 0)
    c = lax.broadcasted_iota(jnp.int32, (tm, tm), 1)
    strict = jnp.where(r > c, 1.0, 0.0).astype(BF16)
    before = jnp.dot(strict, onehot.astype(BF16), preferred_element_type=F32) + carry_ref[...]
    rank1 = jnp.sum(jnp.where(sel1, before, 0.0), axis=-1, keepdims=True).astype(jnp.int32)
    rank2 = jnp.sum(jnp.where(sel2, before, 0.0), axis=-1, keepdims=True).astype(jnp.int32)
    carry_ref[...] = carry_ref[...] + jnp.sum(onehot, axis=0, keepdims=True)

    info_ref[...] = jnp.where(col == 0, i1, jnp.where(col == 1, i2,
                              jnp.where(col == 2, rank1, jnp.where(col == 3, rank2, 0))))
    gate_ref[...] = jnp.where(col == 0, g1, jnp.where(col == 1, g2, 0.0))
    cnt_ref[...] = carry_ref[...]


def router(h, g, rw, rb, tm=256):
    m, d = h.shape
    tm = min(tm, m)
    rw_hi = rw.astype(BF16)
    rw_lo = (rw - rw_hi.astype(F32)).astype(BF16)
    return pl.pallas_call(
        _router_body,
        grid=(m // tm,),
        in_specs=[pl.BlockSpec((tm, d), lambda i: (i, 0)),
                  pl.BlockSpec((1, d), lambda i: (0, 0)),
                  pl.BlockSpec((d, LANES), lambda i: (0, 0)),
                  pl.BlockSpec((d, LANES), lambda i: (0, 0)),
                  pl.BlockSpec((1, LANES), lambda i: (0, 0))],
        out_specs=[pl.BlockSpec((tm, LANES), lambda i: (i, 0)),
                   pl.BlockSpec((tm, LANES), lambda i: (i, 0)),
                   pl.BlockSpec((1, LANES), lambda i: (0, 0))],
        out_shape=[jax.ShapeDtypeStruct((m, LANES), jnp.int32),
                   jax.ShapeDtypeStruct((m, LANES), F32),
                   jax.ShapeDtypeStruct((1, LANES), F32)],
        scratch_shapes=[pltpu.VMEM((1, LANES), F32)],
        compiler_params=_cparams("arbitrary"),
        name="router",
    )(h, g, rw_hi, rw_lo, rb)


def _row_copy(src_hbm, dst_vmem, sem, tok, r):
    return pltpu.make_async_copy(src_hbm.at[pl.ds(tok, 1), :],
                                 dst_vmem.at[pl.ds(r, 1), :], sem)


def _issue_rows(src_hbm, dst_vmem, sem, idx_ref, first, count):
    def body(k, carry):
        r = first + k
        _row_copy(src_hbm, dst_vmem, sem, idx_ref[0, 0, r], r).start()
        return carry

    lax.fori_loop(0, count, body, 0, unroll=8)


def _wait_rows(src_hbm, dst_vmem, sem):
    n = dst_vmem.shape[0]
    pltpu.make_async_copy(src_hbm.at[pl.ds(0, n), :], dst_vmem, sem).wait()


def _expert_body(te_ref, nv_ref, idx_ref, h_hbm, g_ref, gs_ref, wg_ref, wu_ref, wd_ref,
                 o_ref, xbuf_ref, xn_ref, sem, *, issue_steps):
    i = pl.program_id(0)
    j = pl.program_id(1)
    tm = o_ref.shape[0]
    per_step = tm // issue_steps

    @pl.when((i == 0) & (j == 0))
    def _():
        _issue_rows(h_hbm, xbuf_ref, sem, idx_ref, 0, tm)

    @pl.when(j == 0)
    def _():
        _wait_rows(h_hbm, xbuf_ref, sem)
        _rms_rows(xbuf_ref, g_ref, xn_ref)
        o_ref[...] = jnp.zeros_like(o_ref)

    @pl.when((j >= 1) & (j <= issue_steps) & (i + 1 < pl.num_programs(0)))
    def _():
        _issue_rows(h_hbm, xbuf_ref, sem, idx_ref, (j - 1) * per_step, per_step)

    @pl.when(i < nv_ref[0])
    def _():
        _swiglu_step(xn_ref[...], wg_ref, wu_ref, wd_ref, o_ref)

    @pl.when(j == pl.num_programs(1) - 1)
    def _():
        def scale(rows):
            o_ref[rows, :] = o_ref[rows, :] * gs_ref[rows, :]

        _for_row_slabs(o_ref, scale)


def expert_ffn(h, g, src, gs, tile_expert, n_valid, wg, wu, wd, tm, tf=256):
    p = src.shape[0]
    d = h.shape[1]
    f = wg.shape[2]
    nf = f // tf
    nt = p // tm
    issue_steps = min(8, nf - 1)

    def fblk(i, j, nv):
        return jnp.where(i < nv[0], j, nf - 1)

    def idx_blk(i, j, te, nv):
        return (jnp.where((i == 0) & (j == 0), 0, jnp.minimum(i + 1, nt - 1)), 0, 0)

    grid_spec = pltpu.PrefetchScalarGridSpec(
        num_scalar_prefetch=2,
        grid=(nt, nf),
        in_specs=[pl.BlockSpec((1, 1, tm), idx_blk, memory_space=pltpu.SMEM),
                  pl.BlockSpec(memory_space=pl.ANY),
                  pl.BlockSpec((1, d), lambda i, j, te, nv: (0, 0)),
                  pl.BlockSpec((tm, 1), lambda i, j, te, nv: (i, 0)),
                  pl.BlockSpec((None, d, tf), lambda i, j, te, nv: (te[i], 0, fblk(i, j, nv))),
                  pl.BlockSpec((None, d, tf), lambda i, j, te, nv: (te[i], 0, fblk(i, j, nv))),
                  pl.BlockSpec((None, tf, d), lambda i, j, te, nv: (te[i], fblk(i, j, nv), 0))],
        out_specs=pl.BlockSpec((tm, d), lambda i, j, te, nv: (i, 0)),
        scratch_shapes=[pltpu.VMEM((tm, d), F32), pltpu.VMEM((tm, d), BF16),
                        pltpu.SemaphoreType.DMA(())],
    )
    return pl.pallas_call(
        functools.partial(_expert_body, issue_steps=issue_steps),
        grid_spec=grid_spec,
        out_shape=jax.ShapeDtypeStruct((p, d), F32),
        compiler_params=_cparams("arbitrary", "arbitrary"),
        name="moe_experts",
    )(tile_expert, n_valid, src.reshape(nt, 1, tm), h, g, gs, wg, wu, wd)


def moe_dispatch(h, g, rw, rb, wg, wu, wd, tm=512):
    m, d = h.shape
    tm = min(tm, m)
    info, gates, counts = router(h, g, rw, rb)
    e1, e2, r1, r2 = info[:, 0], info[:, 1], info[:, 2], info[:, 3]
    g1, g2 = gates[:, 0], gates[:, 1]
    counts = counts[0, :N_EXPERTS].astype(jnp.int32)

    tiles_e = (counts + tm - 1) // tm
    tile_end = jnp.cumsum(tiles_e)
    row_off = (tile_end - tiles_e) * tm
    n_tiles = (2 * m) // tm + N_EXPERTS
    p = n_tiles * tm
    pos1 = row_off[e1] + r1
    pos2 = row_off[e2] + r2
    pos = jnp.concatenate([pos1, pos2])
    tok = jnp.arange(m, dtype=jnp.int32)
    src = jnp.zeros((p,), jnp.int32).at[pos].set(jnp.concatenate([tok, tok]))
    gs = jnp.zeros((p,), F32).at[pos].set(jnp.concatenate([g1, g2]))
    n_valid = tile_end[-1:].astype(jnp.int32)
    tile_id = jnp.minimum(jnp.arange(n_tiles, dtype=jnp.int32), n_valid[0] - 1)
    tile_expert = jnp.sum(tile_id[:, None] >= tile_end[None, :], axis=1).astype(jnp.int32)

    ys = expert_ffn(h, g, src, gs.reshape(p, 1), tile_expert, n_valid, wg, wu, wd, tm)
    return ys, pos1, pos2


def _ple_body(*refs, combine, last):
    refs = list(refs)
    if combine:
        p1c_ref, p2c_ref, p1n_ref, p2n_ref, ys_hbm = refs[:5]
        refs = refs[5:]
    h_ref, g_ref, gd_ref, gu_ref, p_ref, pw_ref, gn_ref = refs[:7]
    refs = refs[7:]
    if not last:
        wa_ref = refs.pop(0)
    o_ref = refs.pop(0)
    if not last:
        xn_ref, a_ref = refs.pop(0), refs.pop(0)

    h = h_ref[...]
    if combine:
        ab_ref, sem = refs
        i = pl.program_id(0)
        tc = h_ref.shape[0]

        def issue(p1_ref, p2_ref):
            _issue_rows(ys_hbm, ab_ref.at[0], sem.at[0], p1_ref, 0, tc)
            _issue_rows(ys_hbm, ab_ref.at[1], sem.at[1], p2_ref, 0, tc)

        @pl.when(i == 0)
        def _():
            issue(p1c_ref, p2c_ref)

        _wait_rows(ys_hbm, ab_ref.at[0], sem.at[0])
        _wait_rows(ys_hbm, ab_ref.at[1], sem.at[1])
        h = h + ab_ref[0] + ab_ref[1]
        o_ref[...] = h

        @pl.when(i + 1 < pl.num_programs(0))
        def _():
            issue(p1n_ref, p2n_ref)

        h = o_ref[...]

    gn = _rms(h, g_ref[...]).astype(BF16)
    t = jnp.dot(gn, gd_ref[...], preferred_element_type=F32).astype(BF16)
    gate = _sigmoid(jnp.dot(t, gu_ref[...], preferred_element_type=F32))
    emb = jnp.dot(p_ref[...].astype(BF16), pw_ref[...], preferred_element_type=F32)
    hn = h + gate * emb
    if last:
        o_ref[...] = _rms(hn, gn_ref[...])
    else:
        o_ref[...] = hn
        xn = _rms(hn, gn_ref[...]).astype(BF16)
        xn_ref[...] = xn
        a_ref[...] = _gate_proj(xn, wa_ref)


def ple(h, g, gd, gu, p, pw, g_next, wa_next=None, moe_parts=None, tm=256):
    m, d = h.shape
    e = p.shape[1]
    tm = min(tm, m)
    nt = m // tm
    last = wa_next is None
    combine = moe_parts is not None
    row = lambda w: pl.BlockSpec((tm, w), lambda i: (i, 0))
    full = lambda a, b: pl.BlockSpec((a, b), lambda i: (0, 0))
    in_specs, args, scratch = [], [], []
    if combine:
        ys, pos1, pos2 = moe_parts
        cur = pl.BlockSpec((1, 1, tm), lambda i: (i, 0, 0), memory_space=pltpu.SMEM)
        nxt = pl.BlockSpec((1, 1, tm), lambda i: (jnp.minimum(i + 1, nt - 1), 0, 0),
                           memory_space=pltpu.SMEM)
        p1, p2 = pos1.reshape(nt, 1, tm), pos2.reshape(nt, 1, tm)
        in_specs += [cur, cur, nxt, nxt, pl.BlockSpec(memory_space=pl.ANY)]
        args += [p1, p2, p1, p2, ys]
        scratch = [pltpu.VMEM((2, tm, d), F32), pltpu.SemaphoreType.DMA((2,))]
    in_specs += [row(d), full(1, d), full(d, e), full(e, d), row(e), full(e, d), full(1, d)]
    args += [h, g, gd, gu, p, pw, g_next]
    if last:
        out_specs = row(d)
        out_shape = jax.ShapeDtypeStruct((m, d), F32)
    else:
        in_specs.append(_gate_block_spec(d))
        args.append(wa_next)
        out_specs = [row(d), row(d), row(LANES)]
        out_shape = [jax.ShapeDtypeStruct((m, d), F32), jax.ShapeDtypeStruct((m, d), BF16),
                     jax.ShapeDtypeStruct((m, LANES), F32)]
    return pl.pallas_call(
        functools.partial(_ple_body, combine=combine, last=last),
        grid=(nt,), in_specs=in_specs, out_specs=out_specs, out_shape=out_shape,
        scratch_shapes=scratch,
        compiler_params=_cparams("arbitrary" if combine else "parallel"),
        name="ple_final" if last else "ple_next",
    )(*args)


def kernel(x, p, ln_mix_g, w_in, w_a_up, b_a, gla_g, conv_w, conv_g, w_out, ln_ffn_g, ffn_wg, ffn_wu, ffn_wd, router_w, router_b, exp_wg, exp_wu, exp_wd, ln_ple_g, ple_gd, ple_gu, ple_proj, final_g):
    batch, seq, d = x.shape
    depth = w_in.shape[0]
    m = batch * seq
    a0 = GATE_COL
    row2 = lambda v: v.reshape(1, -1)

    h = x.reshape(m, d)
    xn, a = norm_a(h, row2(ln_mix_g[0]), w_in[0])
    out = None
    for i in range(depth):
        w_main = jnp.concatenate([w_in[i][:, :a0], w_in[i][:, a0 + GATE_RANK:]], axis=1).astype(BF16)
        pj = proj(xn, w_main)
        wup = jnp.pad(w_a_up[i], ((0, LANES - GATE_RANK), (0, 0)))
        ya = gla(pj, a, wup, row2(b_a[i]), row2(gla_g[i]), batch, seq)
        cw = jnp.pad(conv_w[i], ((0, 8 - conv_w.shape[1]), (0, 0)))
        yb = sconv(pj, cw, row2(conv_g[i]), batch, seq)
        h = out_proj(ya, yb, w_out[i].astype(BF16), h)

        j = i // 2
        if i % 2 == 0:
            fpad = (-ffn_wg.shape[2]) % FFN_TF
            h = ffn(h, row2(ln_ffn_g[i]),
                    jnp.pad(ffn_wg[j].astype(BF16), ((0, 0), (0, fpad))),
                    jnp.pad(ffn_wu[j].astype(BF16), ((0, 0), (0, fpad))),
                    jnp.pad(ffn_wd[j].astype(BF16), ((0, fpad), (0, 0))), tf=FFN_TF)
            moe_parts = None
        else:
            rw = jnp.pad(router_w[j], ((0, 0), (0, LANES - N_EXPERTS)))
            rb = jnp.pad(router_b[j], (0, LANES - N_EXPERTS), constant_values=NEG_BIG)
            moe_parts = moe_dispatch(h, row2(ln_ffn_g[i]), rw, row2(rb), exp_wg[j].astype(BF16),
                                     exp_wu[j].astype(BF16), exp_wd[j].astype(BF16))

        ple_args = (h, row2(ln_ple_g[i]), ple_gd[i].astype(BF16), ple_gu[i].astype(BF16),
                    p[i].reshape(m, -1), ple_proj[i].astype(BF16))
        if i + 1 < depth:
            h, xn, a = ple(*ple_args, row2(ln_mix_g[i + 1]), w_in[i + 1], moe_parts=moe_parts)
        else:
            out = ple(*ple_args, row2(final_g), moe_parts=moe_parts)
    return out.reshape(batch, seq, d)
```

```python
import functools

import jax
import jax.numpy as jnp
from jax import lax
from jax.experimental import pallas as pl
from jax.experimental.pallas import tpu as pltpu

F32 = jnp.float32
BF16 = jnp.bfloat16
HIGHEST = lax.Precision.HIGHEST

EPS = 1e-6
LANES = 128
GLA_HEADS = 8
GLA_DK = 128
GLA_DV = 256
GATE_RANK = 16
GATE_COL = 2 * GLA_HEADS * GLA_DK + 2 * GLA_HEADS * GLA_DV
GATE_TAU = 16.0
CHUNK = 64
SUB = 16
N_SUB = CHUNK // SUB
CONV_GROUPS = 16
CONV_GROUP_W = 128
N_EXPERTS = 8
LOG2E = 1.4426950408889634
MAX_SUB_DECAY_LOG2 = 100.0
ROW_SLAB = 128
DOWN_SLAB = 512
FFN_TF = 256
EXPERT_TF = 256
OUT_TN = 512
NEG_BIG = -1e30
VMEM_LIMIT = 56 * 1024 * 1024


def _cparams(*sem):
    return pltpu.CompilerParams(dimension_semantics=sem, vmem_limit_bytes=VMEM_LIMIT)


def _rms(x, g):
    ms = jnp.mean(x * x, axis=-1, keepdims=True)
    return x * lax.rsqrt(ms + EPS) * g


def _sigmoid(x):
    return 1.0 / (1.0 + jnp.exp(-x))


def _for_row_slabs(ref, fn):
    n = ref.shape[0]
    slab = min(ROW_SLAB, n)

    def body(c, carry):
        fn(pl.ds(pl.multiple_of(c * slab, slab), slab))
        return carry

    lax.fori_loop(0, n // slab, body, 0)


def _rms_rows(src_ref, g_ref, dst_ref):
    def slab(rows):
        dst_ref[rows, :] = _rms(src_ref[rows, :], g_ref[...]).astype(dst_ref.dtype)

    _for_row_slabs(src_ref, slab)


def _gate_proj(xn, wa_ref):
    a = jnp.dot(xn, wa_ref[...].astype(BF16), preferred_element_type=F32)
    col = lax.broadcasted_iota(jnp.int32, a.shape, 1)
    return jnp.where(col < GATE_RANK, a, 0.0)


def _gate_block_spec(d, layer):
    blk = GATE_COL // LANES
    return pl.BlockSpec((None, d, LANES), lambda *_: (layer, 0, blk))


def _norm_a_body(h_ref, g_ref, wa_ref, xn_ref, a_ref):
    xn = _rms(h_ref[...], g_ref[...]).astype(BF16)
    xn_ref[...] = xn
    a_ref[...] = _gate_proj(xn, wa_ref)


def norm_a(h, g, w_in, layer, tm=256):
    m, d = h.shape
    tm = min(tm, m)
    return pl.pallas_call(
        _norm_a_body,
        grid=(m // tm,),
        in_specs=[pl.BlockSpec((tm, d), lambda i: (i, 0)),
                  pl.BlockSpec((1, d), lambda i: (0, 0)),
                  _gate_block_spec(d, layer)],
        out_specs=[pl.BlockSpec((tm, d), lambda i: (i, 0)),
                   pl.BlockSpec((tm, LANES), lambda i: (i, 0))],
        out_shape=[jax.ShapeDtypeStruct((m, d), BF16),
                   jax.ShapeDtypeStruct((m, LANES), F32)],
        compiler_params=_cparams("parallel"),
        name="norm_a",
    )(h, g, w_in)


def _proj_body(x_ref, w_ref, o_ref):
    o_ref[...] = jnp.dot(x_ref[...], w_ref[...],
                         preferred_element_type=F32).astype(o_ref.dtype)


def proj(x, w, tm=1024, tn=1024):
    m, d = x.shape
    n = w.shape[1]
    tm = min(tm, m)
    return pl.pallas_call(
        _proj_body,
        grid=(m // tm, n // tn),
        in_specs=[pl.BlockSpec((tm, d), lambda i, j: (i, 0)),
                  pl.BlockSpec((d, tn), lambda i, j: (0, j))],
        out_specs=pl.BlockSpec((tm, tn), lambda i, j: (i, j)),
        out_shape=jax.ShapeDtypeStruct((m, n), BF16),
        compiler_params=_cparams("parallel", "arbitrary"),
        name="in_proj",
    )(x, w)


def _gla_body(q_ref, k_ref, v_ref, r_ref, a_ref, wuh_ref, wul_ref, ba_ref, g_ref, y_ref,
              st_ref, bb_ref, qb_ref, u_ref, o_ref, ks_ref, bs_ref, *, n_chunks):
    @pl.when(pl.program_id(2) == 0)
    def _():
        st_ref[...] = jnp.zeros_like(st_ref)

    row = lax.broadcasted_iota(jnp.int32, (CHUNK, CHUNK), 0)
    col = lax.broadcasted_iota(jnp.int32, (CHUNK, CHUNK), 1)
    dloc = col - (row // SUB) * SUB
    dsel = jnp.where((dloc >= 0) & (dloc <= row % SUB), dloc, -1)
    krow = lax.broadcasted_iota(jnp.int32, (CHUNK, GLA_DK), 0)
    scale = GLA_DK ** -0.5

    a_hi = a_ref[...].astype(BF16)
    a_lo = (a_ref[...] - a_hi.astype(F32)).astype(BF16)
    z = (jnp.dot(a_hi, wuh_ref[...], preferred_element_type=F32)
         + jnp.dot(a_lo, wuh_ref[...], preferred_element_type=F32)
         + jnp.dot(a_hi, wul_ref[...], preferred_element_type=F32)) + ba_ref[...]
    csum = (jnp.minimum(z, 0.0) - jnp.log(1.0 + jnp.exp(-jnp.abs(z)))) * (LOG2E / GATE_TAU)
    crow = lax.broadcasted_iota(jnp.int32, csum.shape, 0) % CHUNK
    shift = 1
    while shift < CHUNK:
        csum = csum + jnp.where(crow >= shift, pltpu.roll(csum, shift, axis=0), 0.0)
        shift *= 2
    bb_ref[...] = csum

    sub_decay = jnp.where(crow % SUB == SUB - 1,
                          pltpu.roll(csum, SUB - 1, axis=0) - csum, 0.0)
    matmul_safe = jnp.max(sub_decay) < MAX_SUB_DECAY_LOG2

    def intra(c, slot, use_matmul):
        r0 = pl.multiple_of(c * CHUNK, CHUNK)
        rows = pl.ds(r0, CHUNK)
        qc = q_ref[rows, :].astype(F32) * scale
        kc = k_ref[rows, :].astype(F32)
        vc = v_ref[rows, :]
        bc = bb_ref[rows, :]

        if use_matmul:
            blocks = []
            for i in range(N_SUB):
                sub = slice(i * SUB, (i + 1) * SUB)
                ref_i = bc[i * SUB:i * SUB + 1]
                qs = qc[sub] * jnp.exp2(bc[sub] - ref_i)
                kk = jnp.where(krow < (i + 1) * SUB, kc * jnp.exp2(ref_i - bc), 0.0)
                blocks.append(lax.dot_general(
                    qs.astype(BF16), kk.astype(BF16), (((1,), (1,)), ((), ())),
                    preferred_element_type=F32))
            amat = jnp.where(row >= col, jnp.concatenate(blocks, axis=0), 0.0)
        else:
            ks = ks_ref.at[slot]
            bs = bs_ref.at[slot]
            ks[...] = kc
            bs[...] = bc
            amat = jnp.zeros((CHUNK, CHUNK), F32)
            for j in range(SUB):
                kj = jnp.concatenate(
                    [jnp.broadcast_to(ks[pl.ds(i * SUB + j, 1), :], (SUB, GLA_DK))
                     for i in range(N_SUB)], axis=0)
                bj = jnp.concatenate(
                    [jnp.broadcast_to(bs[pl.ds(i * SUB + j, 1), :], (SUB, GLA_DK))
                     for i in range(N_SUB)], axis=0)
                pj = qc * jnp.exp2(bc - bj) * kj
                amat = jnp.where(dsel == j, jnp.sum(pj, axis=-1, keepdims=True), amat)
            off_rows = [jnp.zeros((SUB, CHUNK), F32)]
            for i in range(1, N_SUB):
                sub = slice(i * SUB, (i + 1) * SUB)
                ref_i = bc[i * SUB:i * SUB + 1]
                qs = qc[sub] * jnp.exp2(bc[sub] - ref_i)
                kk = jnp.where(krow < i * SUB, kc * jnp.exp2(ref_i - bc), 0.0)
                off_rows.append(lax.dot_general(
                    qs.astype(BF16), kk.astype(BF16), (((1,), (1,)), ((), ())),
                    preferred_element_type=F32))
            amat = amat + jnp.concatenate(off_rows, axis=0)

        qb_ref[rows, :] = (qc * jnp.exp2(bc)).astype(BF16)
        b_last = bc[CHUNK - 1:CHUNK]
        kd = (kc * jnp.exp2(b_last - bc)).astype(BF16)
        u_ref[c] = lax.dot_general(vc, kd, (((0,), (0,)), ((), ())),
                                   preferred_element_type=F32)
        return amat.astype(BF16)

    def intra_out(c, amat):
        rows = pl.ds(pl.multiple_of(c * CHUNK, CHUNK), CHUNK)
        o_ref[rows, :] = jnp.dot(amat, v_ref[rows, :], preferred_element_type=F32)

    amats = [intra(c, 0, True) for c in range(n_chunks)]
    for c in range(n_chunks):
        intra_out(c, amats[c])

    @pl.when(jnp.logical_not(matmul_safe))
    def _():
        def pair(i, carry):
            a0 = intra(2 * i, 0, False)
            a1 = intra(2 * i + 1, 1, False)
            intra_out(2 * i, a0)
            intra_out(2 * i + 1, a1)
            return carry

        lax.fori_loop(0, n_chunks // 2, pair, 0)

    st = st_ref[...]
    for c in range(n_chunks):
        rows = slice(c * CHUNK, (c + 1) * CHUNK)
        o_ref[rows, :] += lax.dot_general(qb_ref[rows, :], st.astype(BF16),
                                          (((1,), (1,)), ((), ())),
                                          preferred_element_type=F32)
        st = st * jnp.exp2(bb_ref[(c + 1) * CHUNK - 1:(c + 1) * CHUNK, :]) + u_ref[c]
    st_ref[...] = st

    rg = r_ref[...].astype(F32)
    y_ref[...] = (_rms(o_ref[...], g_ref[...]) * (rg * _sigmoid(rg))).astype(y_ref.dtype)


def gla(pj, a, wup, ba, g, batch, seq, tb=512):
    m = pj.shape[0]
    tb = min(tb, seq)
    nt = seq // tb
    nc = tb // CHUNK
    kq = GLA_HEADS
    vq = (2 * GLA_HEADS * GLA_DK) // GLA_DV
    rq = vq + GLA_HEADS
    tok = lambda b, h, t: b * nt + t
    wup_hi = wup.astype(BF16)
    wup_lo = (wup - wup_hi.astype(F32)).astype(BF16)
    return pl.pallas_call(
        functools.partial(_gla_body, n_chunks=nc),
        grid=(batch, GLA_HEADS, nt),
        in_specs=[
            pl.BlockSpec((tb, GLA_DK), lambda b, h, t: (tok(b, h, t), h)),
            pl.BlockSpec((tb, GLA_DK), lambda b, h, t: (tok(b, h, t), kq + h)),
            pl.BlockSpec((tb, GLA_DV), lambda b, h, t: (tok(b, h, t), vq + h)),
            pl.BlockSpec((tb, GLA_DV), lambda b, h, t: (tok(b, h, t), rq + h)),
            pl.BlockSpec((tb, LANES), lambda b, h, t: (tok(b, h, t), 0)),
            pl.BlockSpec((LANES, GLA_DK), lambda b, h, t: (0, h)),
            pl.BlockSpec((LANES, GLA_DK), lambda b, h, t: (0, h)),
            pl.BlockSpec((1, GLA_DK), lambda b, h, t: (0, h)),
            pl.BlockSpec((1, GLA_DV), lambda b, h, t: (0, h)),
        ],
        out_specs=pl.BlockSpec((tb, GLA_DV), lambda b, h, t: (tok(b, h, t), h)),
        out_shape=jax.ShapeDtypeStruct((m, GLA_HEADS * GLA_DV), BF16),
        scratch_shapes=[pltpu.VMEM((GLA_DV, GLA_DK), F32),
                        pltpu.VMEM((tb, GLA_DK), F32),
                        pltpu.VMEM((tb, GLA_DK), BF16),
                        pltpu.VMEM((nc, GLA_DV, GLA_DK), F32),
                        pltpu.VMEM((tb, GLA_DV), F32),
                        pltpu.VMEM((2, CHUNK, GLA_DK), F32),
                        pltpu.VMEM((2, CHUNK, GLA_DK), F32)],
        compiler_params=_cparams("parallel", "parallel", "arbitrary"),
        name="gla",
    )(pj, pj, pj, pj, a, wup_hi, wup_lo, ba, g)


def _conv_body(b_ref, c_ref, u_ref, w_ref, g_ref, y_ref, tail_ref):
    @pl.when(pl.program_id(1) == 0)
    def _():
        tail_ref[...] = jnp.zeros_like(tail_ref)

    tb = b_ref.shape[0]
    row = lax.broadcasted_iota(jnp.int32, (tb, CONV_GROUP_W), 0)
    for gi in range(CONV_GROUPS):
        cols = slice(gi * CONV_GROUP_W, (gi + 1) * CONV_GROUP_W)
        cu = c_ref[:, cols].astype(F32) * u_ref[:, cols].astype(F32)
        t1 = tail_ref[pl.ds(7, 1), cols]
        t2 = tail_ref[pl.ds(6, 1), cols]
        cu1 = jnp.where(row == 0, t1, pltpu.roll(cu, 1, axis=0))
        cu2 = jnp.where(row == 0, t2, jnp.where(row == 1, t1, pltpu.roll(cu, 2, axis=0)))
        conv = (w_ref[pl.ds(0, 1), cols] * cu2 + w_ref[pl.ds(1, 1), cols] * cu1
                + w_ref[pl.ds(2, 1), cols] * cu)
        yb = b_ref[:, cols].astype(F32) * conv
        y_ref[:, cols] = _rms(yb, g_ref[:, cols]).astype(y_ref.dtype)
        tail_ref[:, cols] = cu[tb - 8:, :]


def sconv(pj, w, g, batch, seq, tb=256):
    m = pj.shape[0]
    cw = CONV_GROUPS * CONV_GROUP_W
    tb = min(tb, seq)
    nt = seq // tb
    first = (2 * GLA_HEADS * GLA_DK + 2 * GLA_HEADS * GLA_DV) // cw
    spec = lambda k: pl.BlockSpec((tb, cw), lambda b, t: (b * nt + t, first + k))
    return pl.pallas_call(
        _conv_body,
        grid=(batch, nt),
        in_specs=[spec(0), spec(1), spec(2),
                  pl.BlockSpec((8, cw), lambda b, t: (0, 0)),
                  pl.BlockSpec((1, cw), lambda b, t: (0, 0))],
        out_specs=pl.BlockSpec((tb, cw), lambda b, t: (b * nt + t, 0)),
        out_shape=jax.ShapeDtypeStruct((m, cw), BF16),
        scratch_shapes=[pltpu.VMEM((8, cw), F32)],
        compiler_params=_cparams("parallel", "arbitrary"),
        name="sconv",
    )(pj, pj, pj, w, g)


def _out_body(ya_ref, yb_ref, wa_ref, wb_ref, h_ref, o_ref):
    acc = jnp.dot(ya_ref[...], wa_ref[...], preferred_element_type=F32)
    acc = acc + jnp.dot(yb_ref[...], wb_ref[...], preferred_element_type=F32)
    o_ref[...] = h_ref[...] + acc


def out_proj(ya, yb, w, h, tm=1024):
    m, ka = ya.shape
    kb = yb.shape[1]
    nn, _, tn = w.shape
    n = nn * tn
    tm = min(tm, m)
    nka = ka // kb
    return pl.pallas_call(
        _out_body,
        grid=(m // tm, nn),
        in_specs=[pl.BlockSpec((tm, ka), lambda i, j: (i, 0)),
                  pl.BlockSpec((tm, kb), lambda i, j: (i, 0)),
                  pl.BlockSpec((None, ka, tn), lambda i, j: (j, 0, 0)),
                  pl.BlockSpec((None, kb, tn), lambda i, j: (j, nka, 0)),
                  pl.BlockSpec((tm, tn), lambda i, j: (i, j))],
        out_specs=pl.BlockSpec((tm, tn), lambda i, j: (i, j)),
        out_shape=jax.ShapeDtypeStruct((m, n), F32),
        compiler_params=_cparams("parallel", "arbitrary"),
        name="out_proj",
    )(ya, yb, w, w, h)


def _swiglu_step(xn, wg_ref, wu_ref, wd_ref, o_ref):
    gg = jnp.dot(xn, wg_ref[...], preferred_element_type=F32)
    uu = jnp.dot(xn, wu_ref[...], preferred_element_type=F32)
    act = (gg * _sigmoid(gg) * uu).astype(BF16)
    d = o_ref.shape[1]
    for n in range(d // DOWN_SLAB):
        cols = slice(n * DOWN_SLAB, (n + 1) * DOWN_SLAB)
        o_ref[:, cols] += jnp.dot(act, wd_ref[:, cols], preferred_element_type=F32)


def _ffn_body(h_hbm, g_ref, wg_ref, wu_ref, wd_ref, o_ref, xn_ref, sem):
    tm = o_ref.shape[0]

    @pl.when(pl.program_id(1) == 0)
    def _():
        r0 = pl.multiple_of(pl.program_id(0) * tm, tm)
        cp = pltpu.make_async_copy(h_hbm.at[pl.ds(r0, tm), :], o_ref, sem)
        cp.start()
        cp.wait()
        _rms_rows(o_ref, g_ref, xn_ref)

    _swiglu_step(xn_ref[...], wg_ref, wu_ref, wd_ref, o_ref)


def _column_blocks(w, tn):
    *lead, k, n = w.shape
    w = w.astype(BF16).reshape(*lead, k, n // tn, tn)
    return jnp.swapaxes(w, -3, -2)


def ffn(h, g, wg, wu, wd, tm=512):
    m, d = h.shape
    nf, _, tf = wg.shape
    tm = min(tm, m)
    return pl.pallas_call(
        _ffn_body,
        grid=(m // tm, nf),
        in_specs=[pl.BlockSpec(memory_space=pl.ANY),
                  pl.BlockSpec((1, d), lambda i, j: (0, 0)),
                  pl.BlockSpec((None, d, tf), lambda i, j: (j, 0, 0)),
                  pl.BlockSpec((None, d, tf), lambda i, j: (j, 0, 0)),
                  pl.BlockSpec((tf, d), lambda i, j: (j, 0))],
        out_specs=pl.BlockSpec((tm, d), lambda i, j: (i, 0)),
        out_shape=jax.ShapeDtypeStruct((m, d), F32),
        scratch_shapes=[pltpu.VMEM((tm, d), BF16), pltpu.SemaphoreType.DMA(())],
        compiler_params=_cparams("parallel", "arbitrary"),
        name="ffn",
    )(h, g, wg, wu, wd)


def _router_body(h_ref, g_ref, rwh_ref, rwl_ref, rb_ref, info_ref, gate_ref, cnt_ref, carry_ref):
    @pl.when(pl.program_id(0) == 0)
    def _():
        carry_ref[...] = jnp.zeros_like(carry_ref)

    tm = h_ref.shape[0]
    hn = _rms(h_ref[...], g_ref[...])
    hi = hn.astype(BF16)
    lo = (hn - hi.astype(F32)).astype(BF16)
    logits = (jnp.dot(hi, rwh_ref[...], preferred_element_type=F32)
              + jnp.dot(lo, rwh_ref[...], preferred_element_type=F32)
              + jnp.dot(hi, rwl_ref[...], preferred_element_type=F32)) + rb_ref[...]
    col = lax.broadcasted_iota(jnp.int32, (tm, LANES), 1)
    m1 = jnp.max(logits, axis=-1, keepdims=True)
    i1 = jnp.min(jnp.where(logits == m1, col, LANES), axis=-1, keepdims=True)
    rest = jnp.where(col == i1, 2.0 * NEG_BIG, logits)
    m2 = jnp.max(rest, axis=-1, keepdims=True)
    i2 = jnp.min(jnp.where(rest == m2, col, LANES), axis=-1, keepdims=True)
    e21 = jnp.exp(m2 - m1)
    g1 = 1.0 / (1.0 + e21)
    g2 = e21 / (1.0 + e21)

    sel1 = col == i1
    sel2 = col == i2
    onehot = jnp.where(sel1, 1.0, jnp.where(sel2, 1.0, 0.0))
    r = lax.broadcasted_iota(jnp.int32, (tm, tm), 0)
    c = lax.broadcasted_iota(jnp.int32, (tm, tm), 1)
    strict = jnp.where(r > c, 1.0, 0.0).astype(BF16)
    before = jnp.dot(strict, onehot.astype(BF16), preferred_element_type=F32) + carry_ref[...]
    rank1 = jnp.sum(jnp.where(sel1, before, 0.0), axis=-1, keepdims=True).astype(jnp.int32)
    rank2 = jnp.sum(jnp.where(sel2, before, 0.0), axis=-1, keepdims=True).astype(jnp.int32)
    carry_ref[...] = carry_ref[...] + jnp.sum(onehot, axis=0, keepdims=True)

    info_ref[...] = jnp.where(col == 0, i1, jnp.where(col == 1, i2,
                              jnp.where(col == 2, rank1, jnp.where(col == 3, rank2, 0))))
    gate_ref[...] = jnp.where(col == 0, g1, jnp.where(col == 1, g2, 0.0))
    cnt_ref[...] = carry_ref[...]


def router(h, g, rw, rb, tm=256):
    m, d = h.shape
    tm = min(tm, m)
    rw_hi = rw.astype(BF16)
    rw_lo = (rw - rw_hi.astype(F32)).astype(BF16)
    return pl.pallas_call(
        _router_body,
        grid=(m // tm,),
        in_specs=[pl.BlockSpec((tm, d), lambda i: (i, 0)),
                  pl.BlockSpec((1, d), lambda i: (0, 0)),
                  pl.BlockSpec((d, LANES), lambda i: (0, 0)),
                  pl.BlockSpec((d, LANES), lambda i: (0, 0)),
                  pl.BlockSpec((1, LANES), lambda i: (0, 0))],
        out_specs=[pl.BlockSpec((tm, LANES), lambda i: (i, 0)),
                   pl.BlockSpec((tm, LANES), lambda i: (i, 0)),
                   pl.BlockSpec((1, LANES), lambda i: (0, 0))],
        out_shape=[jax.ShapeDtypeStruct((m, LANES), jnp.int32),
                   jax.ShapeDtypeStruct((m, LANES), F32),
                   jax.ShapeDtypeStruct((1, LANES), F32)],
        scratch_shapes=[pltpu.VMEM((1, LANES), F32)],
        compiler_params=_cparams("arbitrary"),
        name="router",
    )(h, g, rw_hi, rw_lo, rb)


def _row_copy(src_hbm, dst_vmem, sem, tok, r):
    return pltpu.make_async_copy(src_hbm.at[pl.ds(tok, 1), :],
                                 dst_vmem.at[pl.ds(r, 1), :], sem)


def _issue_rows(src_hbm, dst_vmem, sem, idx_ref, first, count):
    def body(k, carry):
        r = first + k
        _row_copy(src_hbm, dst_vmem, sem, idx_ref[0, 0, r], r).start()
        return carry

    lax.fori_loop(0, count, body, 0, unroll=8)


def _wait_rows(src_hbm, dst_vmem, sem):
    n = dst_vmem.shape[0]
    pltpu.make_async_copy(src_hbm.at[pl.ds(0, n), :], dst_vmem, sem).wait()


def _expert_body(te_ref, nv_ref, idx_ref, h_hbm, g_ref, gs_ref, wg_ref, wu_ref, wd_ref,
                 o_ref, xbuf_ref, xn_ref, sem, *, issue_steps):
    i = pl.program_id(0)
    j = pl.program_id(1)
    tm = o_ref.shape[0]
    per_step = tm // issue_steps

    @pl.when((i == 0) & (j == 0))
    def _():
        _issue_rows(h_hbm, xbuf_ref, sem, idx_ref, 0, tm)

    @pl.when(j == 0)
    def _():
        _wait_rows(h_hbm, xbuf_ref, sem)
        _rms_rows(xbuf_ref, g_ref, xn_ref)
        o_ref[...] = jnp.zeros_like(o_ref)

    @pl.when((j >= 1) & (j <= issue_steps) & (i + 1 < pl.num_programs(0)))
    def _():
        _issue_rows(h_hbm, xbuf_ref, sem, idx_ref, (j - 1) * per_step, per_step)

    @pl.when(i < nv_ref[0])
    def _():
        _swiglu_step(xn_ref[...], wg_ref, wu_ref, wd_ref, o_ref)

    @pl.when(j == pl.num_programs(1) - 1)
    def _():
        def scale(rows):
            o_ref[rows, :] = o_ref[rows, :] * gs_ref[rows, :]

        _for_row_slabs(o_ref, scale)


def expert_ffn(h, g, src, gs, tile_expert, n_valid, wg, wu, wd, tm):
    p = src.shape[0]
    d = h.shape[1]
    _, nf, _, tf = wg.shape
    nt = p // tm
    issue_steps = min(8, nf - 1)

    def fblk(i, j, nv):
        return jnp.where(i < nv[0], j, nf - 1)

    def idx_blk(i, j, te, nv):
        return (jnp.where((i == 0) & (j == 0), 0, jnp.minimum(i + 1, nt - 1)), 0, 0)

    grid_spec = pltpu.PrefetchScalarGridSpec(
        num_scalar_prefetch=2,
        grid=(nt, nf),
        in_specs=[pl.BlockSpec((1, 1, tm), idx_blk, memory_space=pltpu.SMEM),
                  pl.BlockSpec(memory_space=pl.ANY),
                  pl.BlockSpec((1, d), lambda i, j, te, nv: (0, 0)),
                  pl.BlockSpec((tm, 1), lambda i, j, te, nv: (i, 0)),
                  pl.BlockSpec((None, None, d, tf), lambda i, j, te, nv: (te[i], fblk(i, j, nv), 0, 0)),
                  pl.BlockSpec((None, None, d, tf), lambda i, j, te, nv: (te[i], fblk(i, j, nv), 0, 0)),
                  pl.BlockSpec((None, tf, d), lambda i, j, te, nv: (te[i], fblk(i, j, nv), 0))],
        out_specs=pl.BlockSpec((tm, d), lambda i, j, te, nv: (i, 0)),
        scratch_shapes=[pltpu.VMEM((tm, d), F32), pltpu.VMEM((tm, d), BF16),
                        pltpu.SemaphoreType.DMA(())],
    )
    return pl.pallas_call(
        functools.partial(_expert_body, issue_steps=issue_steps),
        grid_spec=grid_spec,
        out_shape=jax.ShapeDtypeStruct((p, d), F32),
        compiler_params=_cparams("arbitrary", "arbitrary"),
        name="moe_experts",
    )(tile_expert, n_valid, src.reshape(nt, 1, tm), h, g, gs, wg, wu, wd)


def moe_dispatch(h, g, rw, rb, wg, wu, wd, tm=512):
    m, d = h.shape
    tm = min(tm, m)
    info, gates, counts = router(h, g, rw, rb)
    e1, e2, r1, r2 = info[:, 0], info[:, 1], info[:, 2], info[:, 3]
    g1, g2 = gates[:, 0], gates[:, 1]
    counts = counts[0, :N_EXPERTS].astype(jnp.int32)

    tiles_e = (counts + tm - 1) // tm
    tile_end = jnp.cumsum(tiles_e)
    row_off = (tile_end - tiles_e) * tm
    n_tiles = (2 * m) // tm + N_EXPERTS
    p = n_tiles * tm
    pos1 = row_off[e1] + r1
    pos2 = row_off[e2] + r2
    pos = jnp.concatenate([pos1, pos2])
    tok = jnp.arange(m, dtype=jnp.int32)
    gate_bits = lax.bitcast_convert_type(jnp.concatenate([g1, g2]), jnp.int32)
    rows = jnp.zeros((p, 2), jnp.int32).at[pos].set(
        jnp.stack([jnp.concatenate([tok, tok]), gate_bits], axis=-1))
    src = rows[:, 0]
    gs = lax.bitcast_convert_type(rows[:, 1], F32)
    n_valid = tile_end[-1:].astype(jnp.int32)
    tile_id = jnp.minimum(jnp.arange(n_tiles, dtype=jnp.int32), n_valid[0] - 1)
    tile_expert = jnp.sum(tile_id[:, None] >= tile_end[None, :], axis=1).astype(jnp.int32)

    ys = expert_ffn(h, g, src, gs.reshape(p, 1), tile_expert, n_valid, wg, wu, wd, tm)
    return ys, pos1, pos2


def _ple_body(*refs, combine, last):
    refs = list(refs)
    if combine:
        p1c_ref, p2c_ref, p1n_ref, p2n_ref, ys_hbm = refs[:5]
        refs = refs[5:]
    h_ref, g_ref, gd_ref, gu_ref, p_ref, pw_ref, gn_ref = refs[:7]
    refs = refs[7:]
    if not last:
        wa_ref = refs.pop(0)
    o_ref = refs.pop(0)
    if not last:
        xn_ref, a_ref = refs.pop(0), refs.pop(0)

    h = h_ref[...]
    if combine:
        ab_ref, sem = refs
        i = pl.program_id(0)
        tc = h_ref.shape[0]

        def issue(p1_ref, p2_ref):
            _issue_rows(ys_hbm, ab_ref.at[0], sem.at[0], p1_ref, 0, tc)
            _issue_rows(ys_hbm, ab_ref.at[1], sem.at[1], p2_ref, 0, tc)

        @pl.when(i == 0)
        def _():
            issue(p1c_ref, p2c_ref)

        _wait_rows(ys_hbm, ab_ref.at[0], sem.at[0])
        _wait_rows(ys_hbm, ab_ref.at[1], sem.at[1])
        h = h + ab_ref[0] + ab_ref[1]
        o_ref[...] = h

        @pl.when(i + 1 < pl.num_programs(0))
        def _():
            issue(p1n_ref, p2n_ref)

        h = o_ref[...]

    gn = _rms(h, g_ref[...]).astype(BF16)
    t = jnp.dot(gn, gd_ref[...], preferred_element_type=F32).astype(BF16)
    gate = _sigmoid(jnp.dot(t, gu_ref[...], preferred_element_type=F32))
    emb = jnp.dot(p_ref[...].astype(BF16), pw_ref[...], preferred_element_type=F32)
    hn = h + gate * emb
    if last:
        o_ref[...] = _rms(hn, gn_ref[...])
    else:
        o_ref[...] = hn
        xn = _rms(hn, gn_ref[...]).astype(BF16)
        xn_ref[...] = xn
        a_ref[...] = _gate_proj(xn, wa_ref)


def ple(h, g, gd, gu, p, pw, g_next, w_in=None, next_layer=None, moe_parts=None, tm=256):
    m, d = h.shape
    e = p.shape[1]
    tm = min(tm, m)
    nt = m // tm
    last = next_layer is None
    combine = moe_parts is not None
    row = lambda w: pl.BlockSpec((tm, w), lambda i: (i, 0))
    full = lambda a, b: pl.BlockSpec((a, b), lambda i: (0, 0))
    in_specs, args, scratch = [], [], []
    if combine:
        ys, pos1, pos2 = moe_parts
        cur = pl.BlockSpec((1, 1, tm), lambda i: (i, 0, 0), memory_space=pltpu.SMEM)
        nxt = pl.BlockSpec((1, 1, tm), lambda i: (jnp.minimum(i + 1, nt - 1), 0, 0),
                           memory_space=pltpu.SMEM)
        p1, p2 = pos1.reshape(nt, 1, tm), pos2.reshape(nt, 1, tm)
        in_specs += [cur, cur, nxt, nxt, pl.BlockSpec(memory_space=pl.ANY)]
        args += [p1, p2, p1, p2, ys]
        scratch = [pltpu.VMEM((2, tm, d), F32), pltpu.SemaphoreType.DMA((2,))]
    in_specs += [row(d), full(1, d), full(d, e), full(e, d), row(e), full(e, d), full(1, d)]
    args += [h, g, gd, gu, p, pw, g_next]
    if last:
        out_specs = row(d)
        out_shape = jax.ShapeDtypeStruct((m, d), F32)
    else:
        in_specs.append(_gate_block_spec(d, next_layer))
        args.append(w_in)
        out_specs = [row(d), row(d), row(LANES)]
        out_shape = [jax.ShapeDtypeStruct((m, d), F32), jax.ShapeDtypeStruct((m, d), BF16),
                     jax.ShapeDtypeStruct((m, LANES), F32)]
    return pl.pallas_call(
        functools.partial(_ple_body, combine=combine, last=last),
        grid=(nt,), in_specs=in_specs, out_specs=out_specs, out_shape=out_shape,
        scratch_shapes=scratch,
        compiler_params=_cparams("arbitrary" if combine else "parallel"),
        name="ple_final" if last else "ple_next",
    )(*args)


def kernel(x, p, ln_mix_g, w_in, w_a_up, b_a, gla_g, conv_w, conv_g, w_out, ln_ffn_g, ffn_wg, ffn_wu, ffn_wd, router_w, router_b, exp_wg, exp_wu, exp_wd, ln_ple_g, ple_gd, ple_gu, ple_proj, final_g):
    batch, seq, d = x.shape
    depth = w_in.shape[0]
    m = batch * seq
    a0 = GATE_COL
    row2 = lambda v: v.reshape(1, -1)

    h = x.reshape(m, d)
    xn, a = norm_a(h, row2(ln_mix_g[0]), w_in, 0)
    out = None
    for i in range(depth):
        w_main = jnp.concatenate([w_in[i][:, :a0], w_in[i][:, a0 + GATE_RANK:]], axis=1).astype(BF16)
        pj = proj(xn, w_main)
        wup = jnp.pad(w_a_up[i], ((0, LANES - GATE_RANK), (0, 0)))
        ya = gla(pj, a, wup, row2(b_a[i]), row2(gla_g[i]), batch, seq)
        cw = jnp.pad(conv_w[i], ((0, 8 - conv_w.shape[1]), (0, 0)))
        yb = sconv(pj, cw, row2(conv_g[i]), batch, seq)
        h = out_proj(ya, yb, _column_blocks(w_out[i], OUT_TN), h)

        j = i // 2
        if i % 2 == 0:
            h = ffn(h, row2(ln_ffn_g[i]), _column_blocks(ffn_wg[j], FFN_TF),
                    _column_blocks(ffn_wu[j], FFN_TF), ffn_wd[j].astype(BF16))
            moe_parts = None
        else:
            rw = jnp.pad(router_w[j], ((0, 0), (0, LANES - N_EXPERTS)))
            rb = jnp.pad(router_b[j], (0, LANES - N_EXPERTS), constant_values=NEG_BIG)
            moe_parts = moe_dispatch(h, row2(ln_ffn_g[i]), rw, row2(rb),
                                     _column_blocks(exp_wg[j], EXPERT_TF),
                                     _column_blocks(exp_wu[j], EXPERT_TF), exp_wd[j].astype(BF16))

        ple_args = (h, row2(ln_ple_g[i]), ple_gd[i].astype(BF16), ple_gu[i].astype(BF16),
                    p[i].reshape(m, -1), ple_proj[i].astype(BF16))
        if i + 1 < depth:
            h, xn, a = ple(*ple_args, row2(ln_mix_g[i + 1]), w_in, i + 1, moe_parts=moe_parts)
        else:
            out = ple(*ple_args, row2(final_g), moe_parts=moe_parts)
    return out.reshape(batch, seq, d)
```

```python
import functools

import jax
import jax.numpy as jnp
from jax import lax
from jax.experimental import pallas as pl
from jax.experimental.pallas import tpu as pltpu

F32 = jnp.float32
BF16 = jnp.bfloat16
HIGHEST = lax.Precision.HIGHEST

EPS = 1e-6
LANES = 128
GLA_HEADS = 8
GLA_DK = 128
GLA_DV = 256
GATE_RANK = 16
GATE_COL = 2 * GLA_HEADS * GLA_DK + 2 * GLA_HEADS * GLA_DV
GATE_TAU = 16.0
CHUNK = 64
SUB = 16
N_SUB = CHUNK // SUB
CONV_GROUPS = 16
CONV_GROUP_W = 128
N_EXPERTS = 8
LOG2E = 1.4426950408889634
MAX_SUB_DECAY_LOG2 = 100.0
ROW_SLAB = 128
DOWN_SLAB = 512
FFN_TF = 256
EXPERT_TF = 256
OUT_TN = 512
NEG_BIG = -1e30
VMEM_LIMIT = 56 * 1024 * 1024


def _cparams(*sem):
    return pltpu.CompilerParams(dimension_semantics=sem, vmem_limit_bytes=VMEM_LIMIT)


def _rms(x, g):
    ms = jnp.mean(x * x, axis=-1, keepdims=True)
    return x * lax.rsqrt(ms + EPS) * g


def _sigmoid(x):
    return 1.0 / (1.0 + jnp.exp(-x))


def _for_row_slabs(ref, fn):
    n = ref.shape[0]
    slab = min(ROW_SLAB, n)

    def body(c, carry):
        fn(pl.ds(pl.multiple_of(c * slab, slab), slab))
        return carry

    lax.fori_loop(0, n // slab, body, 0)


def _rms_rows(src_ref, g_ref, dst_ref):
    def slab(rows):
        dst_ref[rows, :] = _rms(src_ref[rows, :], g_ref[...]).astype(dst_ref.dtype)

    _for_row_slabs(src_ref, slab)


def _gate_proj(xn, wa_ref):
    a = jnp.dot(xn, wa_ref[...].astype(BF16), preferred_element_type=F32)
    col = lax.broadcasted_iota(jnp.int32, a.shape, 1)
    return jnp.where(col < GATE_RANK, a, 0.0)


def _gate_block_spec(d, layer):
    blk = GATE_COL // LANES
    return pl.BlockSpec((None, d, LANES), lambda *_: (layer, 0, blk))


def _norm_a_body(h_ref, g_ref, wa_ref, xn_ref, a_ref):
    xn = _rms(h_ref[...], g_ref[...]).astype(BF16)
    xn_ref[...] = xn
    a_ref[...] = _gate_proj(xn, wa_ref)


def norm_a(h, g, w_in, layer, tm=256):
    m, d = h.shape
    tm = min(tm, m)
    return pl.pallas_call(
        _norm_a_body,
        grid=(m // tm,),
        in_specs=[pl.BlockSpec((tm, d), lambda i: (i, 0)),
                  pl.BlockSpec((1, d), lambda i: (0, 0)),
                  _gate_block_spec(d, layer)],
        out_specs=[pl.BlockSpec((tm, d), lambda i: (i, 0)),
                   pl.BlockSpec((tm, LANES), lambda i: (i, 0))],
        out_shape=[jax.ShapeDtypeStruct((m, d), BF16),
                   jax.ShapeDtypeStruct((m, LANES), F32)],
        compiler_params=_cparams("parallel"),
        name="norm_a",
    )(h, g, w_in)


def _proj_body(x_ref, w_ref, o_ref):
    o_ref[...] = jnp.dot(x_ref[...], w_ref[...],
                         preferred_element_type=F32).astype(o_ref.dtype)


def proj(x, w, tm=1024, tn=1024):
    m, d = x.shape
    n = w.shape[1]
    tm = min(tm, m)
    return pl.pallas_call(
        _proj_body,
        grid=(m // tm, n // tn),
        in_specs=[pl.BlockSpec((tm, d), lambda i, j: (i, 0)),
                  pl.BlockSpec((d, tn), lambda i, j: (0, j))],
        out_specs=pl.BlockSpec((tm, tn), lambda i, j: (i, j)),
        out_shape=jax.ShapeDtypeStruct((m, n), BF16),
        compiler_params=_cparams("parallel", "arbitrary"),
        name="in_proj",
    )(x, w)


def _gla_body(q_ref, k_ref, v_ref, r_ref, a_ref, wuh_ref, wul_ref, ba_ref, g_ref, y_ref,
              st_ref, bb_ref, qb_ref, u_ref, o_ref, ks_ref, bs_ref, *, n_chunks):
    @pl.when(pl.program_id(2) == 0)
    def _():
        st_ref[...] = jnp.zeros_like(st_ref)

    row = lax.broadcasted_iota(jnp.int32, (CHUNK, CHUNK), 0)
    col = lax.broadcasted_iota(jnp.int32, (CHUNK, CHUNK), 1)
    dloc = col - (row // SUB) * SUB
    dsel = jnp.where((dloc >= 0) & (dloc <= row % SUB), dloc, -1)
    krow = lax.broadcasted_iota(jnp.int32, (CHUNK, GLA_DK), 0)
    scale = GLA_DK ** -0.5

    a_hi = a_ref[...].astype(BF16)
    a_lo = (a_ref[...] - a_hi.astype(F32)).astype(BF16)
    z = (jnp.dot(a_hi, wuh_ref[...], preferred_element_type=F32)
         + jnp.dot(a_lo, wuh_ref[...], preferred_element_type=F32)
         + jnp.dot(a_hi, wul_ref[...], preferred_element_type=F32)) + ba_ref[...]
    csum = (jnp.minimum(z, 0.0) - jnp.log(1.0 + jnp.exp(-jnp.abs(z)))) * (LOG2E / GATE_TAU)
    crow = lax.broadcasted_iota(jnp.int32, csum.shape, 0) % CHUNK
    shift = 1
    while shift < CHUNK:
        csum = csum + jnp.where(crow >= shift, pltpu.roll(csum, shift, axis=0), 0.0)
        shift *= 2
    bb_ref[...] = csum

    sub_decay = jnp.where(crow % SUB == SUB - 1,
                          pltpu.roll(csum, SUB - 1, axis=0) - csum, 0.0)
    matmul_safe = jnp.max(sub_decay) < MAX_SUB_DECAY_LOG2

    def intra(c, slot, use_matmul):
        r0 = pl.multiple_of(c * CHUNK, CHUNK)
        rows = pl.ds(r0, CHUNK)
        qc = q_ref[rows, :].astype(F32) * scale
        kc = k_ref[rows, :].astype(F32)
        vc = v_ref[rows, :]
        bc = bb_ref[rows, :]

        if use_matmul:
            blocks = []
            for i in range(N_SUB):
                sub = slice(i * SUB, (i + 1) * SUB)
                ref_i = bc[i * SUB:i * SUB + 1]
                qs = qc[sub] * jnp.exp2(bc[sub] - ref_i)
                kk = jnp.where(krow < (i + 1) * SUB, kc * jnp.exp2(ref_i - bc), 0.0)
                blocks.append(lax.dot_general(
                    qs.astype(BF16), kk.astype(BF16), (((1,), (1,)), ((), ())),
                    preferred_element_type=F32))
            amat = jnp.where(row >= col, jnp.concatenate(blocks, axis=0), 0.0)
        else:
            ks = ks_ref.at[slot]
            bs = bs_ref.at[slot]
            ks[...] = kc
            bs[...] = bc
            amat = jnp.zeros((CHUNK, CHUNK), F32)
            for j in range(SUB):
                kj = jnp.concatenate(
                    [jnp.broadcast_to(ks[pl.ds(i * SUB + j, 1), :], (SUB, GLA_DK))
                     for i in range(N_SUB)], axis=0)
                bj = jnp.concatenate(
                    [jnp.broadcast_to(bs[pl.ds(i * SUB + j, 1), :], (SUB, GLA_DK))
                     for i in range(N_SUB)], axis=0)
                pj = qc * jnp.exp2(bc - bj) * kj
                amat = jnp.where(dsel == j, jnp.sum(pj, axis=-1, keepdims=True), amat)
            off_rows = [jnp.zeros((SUB, CHUNK), F32)]
            for i in range(1, N_SUB):
                sub = slice(i * SUB, (i + 1) * SUB)
                ref_i = bc[i * SUB:i * SUB + 1]
                qs = qc[sub] * jnp.exp2(bc[sub] - ref_i)
                kk = jnp.where(krow < i * SUB, kc * jnp.exp2(ref_i - bc), 0.0)
                off_rows.append(lax.dot_general(
                    qs.astype(BF16), kk.astype(BF16), (((1,), (1,)), ((), ())),
                    preferred_element_type=F32))
            amat = amat + jnp.concatenate(off_rows, axis=0)

        qb_ref[rows, :] = (qc * jnp.exp2(bc)).astype(BF16)
        b_last = bc[CHUNK - 1:CHUNK]
        kd = (kc * jnp.exp2(b_last - bc)).astype(BF16)
        u_ref[c] = lax.dot_general(vc, kd, (((0,), (0,)), ((), ())),
                                   preferred_element_type=F32)
        return amat.astype(BF16)

    def intra_out(c, amat):
        rows = pl.ds(pl.multiple_of(c * CHUNK, CHUNK), CHUNK)
        o_ref[rows, :] = jnp.dot(amat, v_ref[rows, :], preferred_element_type=F32)

    amats = [intra(c, 0, True) for c in range(n_chunks)]
    for c in range(n_chunks):
        intra_out(c, amats[c])

    @pl.when(jnp.logical_not(matmul_safe))
    def _():
        def pair(i, carry):
            a0 = intra(2 * i, 0, False)
            a1 = intra(2 * i + 1, 1, False)
            intra_out(2 * i, a0)
            intra_out(2 * i + 1, a1)
            return carry

        lax.fori_loop(0, n_chunks // 2, pair, 0)

    st = st_ref[...]
    for c in range(n_chunks):
        rows = slice(c * CHUNK, (c + 1) * CHUNK)
        o_ref[rows, :] += lax.dot_general(qb_ref[rows, :], st.astype(BF16),
                                          (((1,), (1,)), ((), ())),
                                          preferred_element_type=F32)
        st = st * jnp.exp2(bb_ref[(c + 1) * CHUNK - 1:(c + 1) * CHUNK, :]) + u_ref[c]
    st_ref[...] = st

    rg = r_ref[...].astype(F32)
    y_ref[...] = (_rms(o_ref[...], g_ref[...]) * (rg * _sigmoid(rg))).astype(y_ref.dtype)


def gla(pj, a, wup, ba, g, batch, seq, tb=512):
    m = pj.shape[0]
    tb = min(tb, seq)
    nt = seq // tb
    nc = tb // CHUNK
    kq = GLA_HEADS
    vq = (2 * GLA_HEADS * GLA_DK) // GLA_DV
    rq = vq + GLA_HEADS
    tok = lambda b, h, t: b * nt + t
    wup_hi = wup.astype(BF16)
    wup_lo = (wup - wup_hi.astype(F32)).astype(BF16)
    return pl.pallas_call(
        functools.partial(_gla_body, n_chunks=nc),
        grid=(batch, GLA_HEADS, nt),
        in_specs=[
            pl.BlockSpec((tb, GLA_DK), lambda b, h, t: (tok(b, h, t), h)),
            pl.BlockSpec((tb, GLA_DK), lambda b, h, t: (tok(b, h, t), kq + h)),
            pl.BlockSpec((tb, GLA_DV), lambda b, h, t: (tok(b, h, t), vq + h)),
            pl.BlockSpec((tb, GLA_DV), lambda b, h, t: (tok(b, h, t), rq + h)),
            pl.BlockSpec((tb, LANES), lambda b, h, t: (tok(b, h, t), 0)),
            pl.BlockSpec((LANES, GLA_DK), lambda b, h, t: (0, h)),
            pl.BlockSpec((LANES, GLA_DK), lambda b, h, t: (0, h)),
            pl.BlockSpec((1, GLA_DK), lambda b, h, t: (0, h)),
            pl.BlockSpec((1, GLA_DV), lambda b, h, t: (0, h)),
        ],
        out_specs=pl.BlockSpec((tb, GLA_DV), lambda b, h, t: (tok(b, h, t), h)),
        out_shape=jax.ShapeDtypeStruct((m, GLA_HEADS * GLA_DV), BF16),
        scratch_shapes=[pltpu.VMEM((GLA_DV, GLA_DK), F32),
                        pltpu.VMEM((tb, GLA_DK), F32),
                        pltpu.VMEM((tb, GLA_DK), BF16),
                        pltpu.VMEM((nc, GLA_DV, GLA_DK), F32),
                        pltpu.VMEM((tb, GLA_DV), F32),
                        pltpu.VMEM((2, CHUNK, GLA_DK), F32),
                        pltpu.VMEM((2, CHUNK, GLA_DK), F32)],
        compiler_params=_cparams("parallel", "parallel", "arbitrary"),
        name="gla",
    )(pj, pj, pj, pj, a, wup_hi, wup_lo, ba, g)


def _conv_body(b_ref, c_ref, u_ref, w_ref, g_ref, y_ref, tail_ref):
    @pl.when(pl.program_id(1) == 0)
    def _():
        tail_ref[...] = jnp.zeros_like(tail_ref)

    tb = b_ref.shape[0]
    row = lax.broadcasted_iota(jnp.int32, (tb, CONV_GROUP_W), 0)
    for gi in range(CONV_GROUPS):
        cols = slice(gi * CONV_GROUP_W, (gi + 1) * CONV_GROUP_W)
        cu = c_ref[:, cols].astype(F32) * u_ref[:, cols].astype(F32)
        t1 = tail_ref[pl.ds(7, 1), cols]
        t2 = tail_ref[pl.ds(6, 1), cols]
        cu1 = jnp.where(row == 0, t1, pltpu.roll(cu, 1, axis=0))
        cu2 = jnp.where(row == 0, t2, jnp.where(row == 1, t1, pltpu.roll(cu, 2, axis=0)))
        conv = (w_ref[pl.ds(0, 1), cols] * cu2 + w_ref[pl.ds(1, 1), cols] * cu1
                + w_ref[pl.ds(2, 1), cols] * cu)
        yb = b_ref[:, cols].astype(F32) * conv
        y_ref[:, cols] = _rms(yb, g_ref[:, cols]).astype(y_ref.dtype)
        tail_ref[:, cols] = cu[tb - 8:, :]


def sconv(pj, w, g, batch, seq, tb=256):
    m = pj.shape[0]
    cw = CONV_GROUPS * CONV_GROUP_W
    tb = min(tb, seq)
    nt = seq // tb
    first = (2 * GLA_HEADS * GLA_DK + 2 * GLA_HEADS * GLA_DV) // cw
    spec = lambda k: pl.BlockSpec((tb, cw), lambda b, t: (b * nt + t, first + k))
    return pl.pallas_call(
        _conv_body,
        grid=(batch, nt),
        in_specs=[spec(0), spec(1), spec(2),
                  pl.BlockSpec((8, cw), lambda b, t: (0, 0)),
                  pl.BlockSpec((1, cw), lambda b, t: (0, 0))],
        out_specs=pl.BlockSpec((tb, cw), lambda b, t: (b * nt + t, 0)),
        out_shape=jax.ShapeDtypeStruct((m, cw), BF16),
        scratch_shapes=[pltpu.VMEM((8, cw), F32)],
        compiler_params=_cparams("parallel", "arbitrary"),
        name="sconv",
    )(pj, pj, pj, w, g)


def _out_body(ya_ref, yb_ref, wa_ref, wb_ref, h_ref, o_ref):
    acc = jnp.dot(ya_ref[...], wa_ref[...], preferred_element_type=F32)
    acc = acc + jnp.dot(yb_ref[...], wb_ref[...], preferred_element_type=F32)
    o_ref[...] = h_ref[...] + acc


def out_proj(ya, yb, w, h, tm=1024, tn=OUT_TN):
    m, ka = ya.shape
    kb = yb.shape[1]
    n = w.shape[1]
    tm = min(tm, m)
    nka = ka // kb
    return pl.pallas_call(
        _out_body,
        grid=(m // tm, n // tn),
        in_specs=[pl.BlockSpec((tm, ka), lambda i, j: (i, 0)),
                  pl.BlockSpec((tm, kb), lambda i, j: (i, 0)),
                  pl.BlockSpec((ka, tn), lambda i, j: (0, j)),
                  pl.BlockSpec((kb, tn), lambda i, j: (nka, j)),
                  pl.BlockSpec((tm, tn), lambda i, j: (i, j))],
        out_specs=pl.BlockSpec((tm, tn), lambda i, j: (i, j)),
        out_shape=jax.ShapeDtypeStruct((m, n), F32),
        compiler_params=_cparams("parallel", "arbitrary"),
        name="out_proj",
    )(ya, yb, w, w, h)


def _swiglu_step(xn, wg_ref, wu_ref, wd_ref, o_ref):
    gg = jnp.dot(xn, wg_ref[...], preferred_element_type=F32)
    uu = jnp.dot(xn, wu_ref[...], preferred_element_type=F32)
    act = (gg * _sigmoid(gg) * uu).astype(BF16)
    d = o_ref.shape[1]
    for n in range(d // DOWN_SLAB):
        cols = slice(n * DOWN_SLAB, (n + 1) * DOWN_SLAB)
        o_ref[:, cols] += jnp.dot(act, wd_ref[:, cols], preferred_element_type=F32)


def _ffn_body(h_hbm, g_ref, wg_ref, wu_ref, wd_ref, o_ref, xn_ref, sem):
    tm = o_ref.shape[0]

    @pl.when(pl.program_id(1) == 0)
    def _():
        r0 = pl.multiple_of(pl.program_id(0) * tm, tm)
        cp = pltpu.make_async_copy(h_hbm.at[pl.ds(r0, tm), :], o_ref, sem)
        cp.start()
        cp.wait()
        _rms_rows(o_ref, g_ref, xn_ref)

    _swiglu_step(xn_ref[...], wg_ref, wu_ref, wd_ref, o_ref)


def ffn(h, g, wg, wu, wd, tm=512, tf=FFN_TF):
    m, d = h.shape
    f = wg.shape[1]
    assert f % tf == 0 and d % DOWN_SLAB == 0
    tm = min(tm, m)
    return pl.pallas_call(
        _ffn_body,
        grid=(m // tm, f // tf),
        in_specs=[pl.BlockSpec(memory_space=pl.ANY),
                  pl.BlockSpec((1, d), lambda i, j: (0, 0)),
                  pl.BlockSpec((d, tf), lambda i, j: (0, j)),
                  pl.BlockSpec((d, tf), lambda i, j: (0, j)),
                  pl.BlockSpec((tf, d), lambda i, j: (j, 0))],
        out_specs=pl.BlockSpec((tm, d), lambda i, j: (i, 0)),
        out_shape=jax.ShapeDtypeStruct((m, d), F32),
        scratch_shapes=[pltpu.VMEM((tm, d), BF16), pltpu.SemaphoreType.DMA(())],
        compiler_params=_cparams("parallel", "arbitrary"),
        name="ffn",
    )(h, g, wg, wu, wd)


def _router_body(h_ref, g_ref, rwh_ref, rwl_ref, rb_ref, info_ref, gate_ref, cnt_ref, carry_ref):
    @pl.when(pl.program_id(0) == 0)
    def _():
        carry_ref[...] = jnp.zeros_like(carry_ref)

    tm = h_ref.shape[0]
    hn = _rms(h_ref[...], g_ref[...])
    hi = hn.astype(BF16)
    lo = (hn - hi.astype(F32)).astype(BF16)
    logits = (jnp.dot(hi, rwh_ref[...], preferred_element_type=F32)
              + jnp.dot(lo, rwh_ref[...], preferred_element_type=F32)
              + jnp.dot(hi, rwl_ref[...], preferred_element_type=F32)) + rb_ref[...]
    col = lax.broadcasted_iota(jnp.int32, (tm, LANES), 1)
    m1 = jnp.max(logits, axis=-1, keepdims=True)
    i1 = jnp.min(jnp.where(logits == m1, col, LANES), axis=-1, keepdims=True)
    rest = jnp.where(col == i1, 2.0 * NEG_BIG, logits)
    m2 = jnp.max(rest, axis=-1, keepdims=True)
    i2 = jnp.min(jnp.where(rest == m2, col, LANES), axis=-1, keepdims=True)
    e21 = jnp.exp(m2 - m1)
    g1 = 1.0 / (1.0 + e21)
    g2 = e21 / (1.0 + e21)

    sel1 = col == i1
    sel2 = col == i2
    onehot = jnp.where(sel1, 1.0, jnp.where(sel2, 1.0, 0.0))
    r = lax.broadcasted_iota(jnp.int32, (tm, tm), 0)
    c = lax.broadcasted_iota(jnp.int32, (tm, tm), 1)
    strict = jnp.where(r > c, 1.0, 0.0).astype(BF16)
    before = jnp.dot(strict, onehot.astype(BF16), preferred_element_type=F32) + carry_ref[...]
    rank1 = jnp.sum(jnp.where(sel1, before, 0.0), axis=-1, keepdims=True).astype(jnp.int32)
    rank2 = jnp.sum(jnp.where(sel2, before, 0.0), axis=-1, keepdims=True).astype(jnp.int32)
    carry_ref[...] = carry_ref[...] + jnp.sum(onehot, axis=0, keepdims=True)

    info_ref[...] = jnp.where(col == 0, i1, jnp.where(col == 1, i2,
                              jnp.where(col == 2, rank1, jnp.where(col == 3, rank2, 0))))
    gate_ref[...] = jnp.where(col == 0, g1, jnp.where(col == 1, g2, 0.0))
    cnt_ref[...] = carry_ref[...]


def router(h, g, rw, rb, tm=256):
    m, d = h.shape
    tm = min(tm, m)
    rw_hi = rw.astype(BF16)
    rw_lo = (rw - rw_hi.astype(F32)).astype(BF16)
    return pl.pallas_call(
        _router_body,
        grid=(m // tm,),
        in_specs=[pl.BlockSpec((tm, d), lambda i: (i, 0)),
                  pl.BlockSpec((1, d), lambda i: (0, 0)),
                  pl.BlockSpec((d, LANES), lambda i: (0, 0)),
                  pl.BlockSpec((d, LANES), lambda i: (0, 0)),
                  pl.BlockSpec((1, LANES), lambda i: (0, 0))],
        out_specs=[pl.BlockSpec((tm, LANES), lambda i: (i, 0)),
                   pl.BlockSpec((tm, LANES), lambda i: (i, 0)),
                   pl.BlockSpec((1, LANES), lambda i: (0, 0))],
        out_shape=[jax.ShapeDtypeStruct((m, LANES), jnp.int32),
                   jax.ShapeDtypeStruct((m, LANES), F32),
                   jax.ShapeDtypeStruct((1, LANES), F32)],
        scratch_shapes=[pltpu.VMEM((1, LANES), F32)],
        compiler_params=_cparams("arbitrary"),
        name="router",
    )(h, g, rw_hi, rw_lo, rb)


def _row_copy(src_hbm, dst_vmem, sem, tok, r):
    return pltpu.make_async_copy(src_hbm.at[pl.ds(tok, 1), :],
                                 dst_vmem.at[pl.ds(r, 1), :], sem)


def _issue_rows(src_hbm, dst_vmem, sem, idx_ref, first, count):
    def body(k, carry):
        r = first + k
        _row_copy(src_hbm, dst_vmem, sem, idx_ref[0, 0, r], r).start()
        return carry

    lax.fori_loop(0, count, body, 0, unroll=8)


def _wait_rows(src_hbm, dst_vmem, sem):
    n = dst_vmem.shape[0]
    pltpu.make_async_copy(src_hbm.at[pl.ds(0, n), :], dst_vmem, sem).wait()


def _expert_body(te_ref, nv_ref, idx_ref, h_hbm, g_ref, gs_ref, wg_ref, wu_ref, wd_ref,
                 o_ref, xbuf_ref, xn_ref, sem, *, issue_steps):
    i = pl.program_id(0)
    j = pl.program_id(1)
    tm = o_ref.shape[0]
    per_step = tm // issue_steps

    @pl.when((i == 0) & (j == 0))
    def _():
        _issue_rows(h_hbm, xbuf_ref, sem, idx_ref, 0, tm)

    @pl.when(j == 0)
    def _():
        _wait_rows(h_hbm, xbuf_ref, sem)
        _rms_rows(xbuf_ref, g_ref, xn_ref)
        o_ref[...] = jnp.zeros_like(o_ref)

    @pl.when((j >= 1) & (j <= issue_steps) & (i + 1 < pl.num_programs(0)))
    def _():
        _issue_rows(h_hbm, xbuf_ref, sem, idx_ref, (j - 1) * per_step, per_step)

    @pl.when(i < nv_ref[0])
    def _():
        _swiglu_step(xn_ref[...], wg_ref, wu_ref, wd_ref, o_ref)

    @pl.when(j == pl.num_programs(1) - 1)
    def _():
        def scale(rows):
            o_ref[rows, :] = o_ref[rows, :] * gs_ref[rows, :]

        _for_row_slabs(o_ref, scale)


def expert_ffn(h, g, src, gs, tile_expert, n_valid, wg, wu, wd, tm, tf=EXPERT_TF):
    p = src.shape[0]
    d = h.shape[1]
    nf = wg.shape[2] // tf
    nt = p // tm
    issue_steps = min(8, nf - 1)

    def fblk(i, j, nv):
        return jnp.where(i < nv[0], j, nf - 1)

    def idx_blk(i, j, te, nv):
        return (jnp.where((i == 0) & (j == 0), 0, jnp.minimum(i + 1, nt - 1)), 0, 0)

    grid_spec = pltpu.PrefetchScalarGridSpec(
        num_scalar_prefetch=2,
        grid=(nt, nf),
        in_specs=[pl.BlockSpec((1, 1, tm), idx_blk, memory_space=pltpu.SMEM),
                  pl.BlockSpec(memory_space=pl.ANY),
                  pl.BlockSpec((1, d), lambda i, j, te, nv: (0, 0)),
                  pl.BlockSpec((tm, 1), lambda i, j, te, nv: (i, 0)),
                  pl.BlockSpec((None, d, tf), lambda i, j, te, nv: (te[i], 0, fblk(i, j, nv))),
                  pl.BlockSpec((None, d, tf), lambda i, j, te, nv: (te[i], 0, fblk(i, j, nv))),
                  pl.BlockSpec((None, tf, d), lambda i, j, te, nv: (te[i], fblk(i, j, nv), 0))],
        out_specs=pl.BlockSpec((tm, d), lambda i, j, te, nv: (i, 0)),
        scratch_shapes=[pltpu.VMEM((tm, d), F32), pltpu.VMEM((tm, d), BF16),
                        pltpu.SemaphoreType.DMA(())],
    )
    return pl.pallas_call(
        functools.partial(_expert_body, issue_steps=issue_steps),
        grid_spec=grid_spec,
        out_shape=jax.ShapeDtypeStruct((p, d), F32),
        compiler_params=_cparams("arbitrary", "arbitrary"),
        name="moe_experts",
    )(tile_expert, n_valid, src.reshape(nt, 1, tm), h, g, gs, wg, wu, wd)


def moe_dispatch(h, g, rw, rb, wg, wu, wd, tm=512):
    m, d = h.shape
    tm = min(tm, m)
    info, gates, counts = router(h, g, rw, rb)
    e1, e2, r1, r2 = info[:, 0], info[:, 1], info[:, 2], info[:, 3]
    g1, g2 = gates[:, 0], gates[:, 1]
    counts = counts[0, :N_EXPERTS].astype(jnp.int32)

    tiles_e = (counts + tm - 1) // tm
    tile_end = jnp.cumsum(tiles_e)
    row_off = (tile_end - tiles_e) * tm
    n_tiles = (2 * m) // tm + N_EXPERTS
    p = n_tiles * tm
    pos1 = row_off[e1] + r1
    pos2 = row_off[e2] + r2
    pos = jnp.concatenate([pos1, pos2])
    tok = jnp.arange(m, dtype=jnp.int32)
    gate_bits = lax.bitcast_convert_type(jnp.concatenate([g1, g2]), jnp.int32)
    rows = jnp.zeros((p, 2), jnp.int32).at[pos].set(
        jnp.stack([jnp.concatenate([tok, tok]), gate_bits], axis=-1))
    src = rows[:, 0]
    gs = lax.bitcast_convert_type(rows[:, 1], F32)
    n_valid = tile_end[-1:].astype(jnp.int32)
    tile_id = jnp.minimum(jnp.arange(n_tiles, dtype=jnp.int32), n_valid[0] - 1)
    tile_expert = jnp.sum(tile_id[:, None] >= tile_end[None, :], axis=1).astype(jnp.int32)

    ys = expert_ffn(h, g, src, gs.reshape(p, 1), tile_expert, n_valid, wg, wu, wd, tm)
    return ys, pos1, pos2


def _ple_body(*refs, combine, last):
    refs = list(refs)
    if combine:
        p1c_ref, p2c_ref, p1n_ref, p2n_ref, ys_hbm = refs[:5]
        refs = refs[5:]
    h_ref, g_ref, gd_ref, gu_ref, p_ref, pw_ref, gn_ref = refs[:7]
    refs = refs[7:]
    if not last:
        wa_ref = refs.pop(0)
    o_ref = refs.pop(0)
    if not last:
        xn_ref, a_ref = refs.pop(0), refs.pop(0)

    h = h_ref[...]
    if combine:
        ab_ref, sem = refs
        i = pl.program_id(0)
        tc = h_ref.shape[0]

        def issue(p1_ref, p2_ref):
            _issue_rows(ys_hbm, ab_ref.at[0], sem.at[0], p1_ref, 0, tc)
            _issue_rows(ys_hbm, ab_ref.at[1], sem.at[1], p2_ref, 0, tc)

        @pl.when(i == 0)
        def _():
            issue(p1c_ref, p2c_ref)

        _wait_rows(ys_hbm, ab_ref.at[0], sem.at[0])
        _wait_rows(ys_hbm, ab_ref.at[1], sem.at[1])
        h = h + ab_ref[0] + ab_ref[1]
        o_ref[...] = h

        @pl.when(i + 1 < pl.num_programs(0))
        def _():
            issue(p1n_ref, p2n_ref)

        h = o_ref[...]

    gn = _rms(h, g_ref[...]).astype(BF16)
    t = jnp.dot(gn, gd_ref[...], preferred_element_type=F32).astype(BF16)
    gate = _sigmoid(jnp.dot(t, gu_ref[...], preferred_element_type=F32))
    emb = jnp.dot(p_ref[...].astype(BF16), pw_ref[...], preferred_element_type=F32)
    hn = h + gate * emb
    if last:
        o_ref[...] = _rms(hn, gn_ref[...])
    else:
        o_ref[...] = hn
        xn = _rms(hn, gn_ref[...]).astype(BF16)
        xn_ref[...] = xn
        a_ref[...] = _gate_proj(xn, wa_ref)


def ple(h, g, gd, gu, p, pw, g_next, w_in=None, next_layer=None, moe_parts=None, tm=256):
    m, d = h.shape
    e = p.shape[1]
    tm = min(tm, m)
    nt = m // tm
    last = next_layer is None
    combine = moe_parts is not None
    row = lambda w: pl.BlockSpec((tm, w), lambda i: (i, 0))
    full = lambda a, b: pl.BlockSpec((a, b), lambda i: (0, 0))
    in_specs, args, scratch = [], [], []
    if combine:
        ys, pos1, pos2 = moe_parts
        cur = pl.BlockSpec((1, 1, tm), lambda i: (i, 0, 0), memory_space=pltpu.SMEM)
        nxt = pl.BlockSpec((1, 1, tm), lambda i: (jnp.minimum(i + 1, nt - 1), 0, 0),
                           memory_space=pltpu.SMEM)
        p1, p2 = pos1.reshape(nt, 1, tm), pos2.reshape(nt, 1, tm)
        in_specs += [cur, cur, nxt, nxt, pl.BlockSpec(memory_space=pl.ANY)]
        args += [p1, p2, p1, p2, ys]
        scratch = [pltpu.VMEM((2, tm, d), F32), pltpu.SemaphoreType.DMA((2,))]
    in_specs += [row(d), full(1, d), full(d, e), full(e, d), row(e), full(e, d), full(1, d)]
    args += [h, g, gd, gu, p, pw, g_next]
    if last:
        out_specs = row(d)
        out_shape = jax.ShapeDtypeStruct((m, d), F32)
    else:
        in_specs.append(_gate_block_spec(d, next_layer))
        args.append(w_in)
        out_specs = [row(d), row(d), row(LANES)]
        out_shape = [jax.ShapeDtypeStruct((m, d), F32), jax.ShapeDtypeStruct((m, d), BF16),
                     jax.ShapeDtypeStruct((m, LANES), F32)]
    return pl.pallas_call(
        functools.partial(_ple_body, combine=combine, last=last),
        grid=(nt,), in_specs=in_specs, out_specs=out_specs, out_shape=out_shape,
        scratch_shapes=scratch,
        compiler_params=_cparams("arbitrary" if combine else "parallel"),
        name="ple_final" if last else "ple_next",
    )(*args)


def kernel(x, p, ln_mix_g, w_in, w_a_up, b_a, gla_g, conv_w, conv_g, w_out, ln_ffn_g, ffn_wg, ffn_wu, ffn_wd, router_w, router_b, exp_wg, exp_wu, exp_wd, ln_ple_g, ple_gd, ple_gu, ple_proj, final_g):
    batch, seq, d = x.shape
    depth = w_in.shape[0]
    m = batch * seq
    a0 = GATE_COL
    row2 = lambda v: v.reshape(1, -1)

    h = x.reshape(m, d)
    xn, a = norm_a(h, row2(ln_mix_g[0]), w_in, 0)
    out = None
    for i in range(depth):
        w_main = jnp.concatenate([w_in[i][:, :a0], w_in[i][:, a0 + GATE_RANK:]], axis=1).astype(BF16)
        pj = proj(xn, w_main)
        wup = jnp.pad(w_a_up[i], ((0, LANES - GATE_RANK), (0, 0)))
        ya = gla(pj, a, wup, row2(b_a[i]), row2(gla_g[i]), batch, seq)
        cw = jnp.pad(conv_w[i], ((0, 8 - conv_w.shape[1]), (0, 0)))
        yb = sconv(pj, cw, row2(conv_g[i]), batch, seq)
        h = out_proj(ya, yb, w_out[i].astype(BF16), h)

        j = i // 2
        if i % 2 == 0:
            h = ffn(h, row2(ln_ffn_g[i]), ffn_wg[j].astype(BF16), ffn_wu[j].astype(BF16),
                    ffn_wd[j].astype(BF16))
            moe_parts = None
        else:
            rw = jnp.pad(router_w[j], ((0, 0), (0, LANES - N_EXPERTS)))
            rb = jnp.pad(router_b[j], (0, LANES - N_EXPERTS), constant_values=NEG_BIG)
            moe_parts = moe_dispatch(h, row2(ln_ffn_g[i]), rw, row2(rb), exp_wg[j].astype(BF16),
                                     exp_wu[j].astype(BF16), exp_wd[j].astype(BF16))

        ple_args = (h, row2(ln_ple_g[i]), ple_gd[i].astype(BF16), ple_gu[i].astype(BF16),
                    p[i].reshape(m, -1), ple_proj[i].astype(BF16))
        if i + 1 < depth:
            h, xn, a = ple(*ple_args, row2(ln_mix_g[i + 1]), w_in, i + 1, moe_parts=moe_parts)
        else:
            out = ple(*ple_args, row2(final_g), moe_parts=moe_parts)
    return out.reshape(batch, seq, d)
```
